```python
import math
import jax
import jax.numpy as jnp
from jax import lax
import numpy as np

D_MODEL = 2048
BATCH = 2
SEQ = 16384
DEPTH = 2

GRID_W = 64
HEAD_DIM = 128
NA_HEADS = 4
NA_WIN_R = 8
NA_WIN_C = 16
DA_HEADS = 4
DA_QK_DIM = 64
DA_V_DIM = 2 * DA_QK_DIM
DA_BLOCK = 128
GDN_HEADS = 8
GDN_DK = 128
GDN_DV = 128
GDN_CONV = 5
GDN_CHUNK = 64
N_BRANCHES = 3
FFN_HIDDEN = (((8 * D_MODEL + 2) // 3 + 255) // 256) * 256
RMS_EPS = 1e-6

NA_W = NA_HEADS * HEAD_DIM
DA_QK_W = DA_HEADS * 2 * DA_QK_DIM
DA_V_W = DA_HEADS * DA_V_DIM
GDN_K_W = GDN_HEADS * GDN_DK
GDN_V_W = GDN_HEADS * GDN_DV
GDN_CONV_CH = 2 * GDN_K_W + GDN_V_W
IN_SIZES = (NA_W, NA_W, NA_W, DA_QK_W, DA_QK_W, DA_V_W, GDN_K_W, GDN_K_W, GDN_V_W, GDN_V_W, 2 * GDN_HEADS, 2 * GDN_HEADS)
IN_WIDTH = sum(IN_SIZES)

kernel_name = "hybrid_natten_diffattn_gdn_block"


def _rms(x, g):
    xf = x.astype(jnp.float32)
    y = xf * lax.rsqrt(jnp.mean(xf * xf, axis=-1, keepdims=True) + RMS_EPS)
    return (y * g.astype(jnp.float32)).astype(x.dtype)


def _l2norm(x):
    xf = x.astype(jnp.float32)
    return xf * lax.rsqrt(jnp.sum(xf * xf, axis=-1, keepdims=True) + RMS_EPS)


def _split_cols(u):
    parts, start = [], 0
    for size in IN_SIZES:
        parts.append(u[..., start:start + size])
        start += size
    return parts


def _neighbourhood_attention(q, k, v, rpb):
    b, s, h, d = q.shape
    rows = s // GRID_W
    kr = min(NA_WIN_R, rows)
    kc = NA_WIN_C
    qg = q.reshape(b, rows, GRID_W, h, d)
    kg = k.reshape(b, rows, GRID_W, h, d)
    vg = v.reshape(b, rows, GRID_W, h, d)
    col = jnp.arange(GRID_W)
    col_start = jnp.clip(col - kc // 2, 0, GRID_W - kc)
    col_idx = col_start[:, None] + jnp.arange(kc)[None, :]
    col_bias_idx = col_idx - col[:, None] + (NA_WIN_C - 1)
    scale = d ** -0.5

    def row_block(args):
        q_row, r = args
        r_start = jnp.clip(r - kr // 2, 0, rows - kr)
        k_band = lax.dynamic_slice_in_dim(kg, r_start, kr, axis=1)
        v_band = lax.dynamic_slice_in_dim(vg, r_start, kr, axis=1)
        k_nb = k_band[:, :, col_idx]
        v_nb = v_band[:, :, col_idx]
        row_bias_idx = r_start + jnp.arange(kr) - r + (NA_WIN_R - 1)
        bias = rpb[:, row_bias_idx][:, :, col_bias_idx]
        sc = jnp.einsum('bwhd,brwchd->bhwrc', q_row, k_nb).astype(jnp.float32) * scale
        sc = sc + bias.transpose(0, 2, 1, 3)[None].astype(jnp.float32)
        p = jax.nn.softmax(sc.reshape(b, h, GRID_W, kr * kc), axis=-1).reshape(b, h, GRID_W, kr, kc)
        return jnp.einsum('bhwrc,brwchd->bwhd', p.astype(v.dtype), v_nb)

    out = lax.map(row_block, (jnp.moveaxis(qg, 1, 0), jnp.arange(rows)))
    return jnp.moveaxis(out, 0, 1).reshape(b, s, h, d)


def _diff_attention(q, k, v, lam, lam_init, subln_g):
    b, s, h, _, dq = q.shape
    nblk = s // DA_BLOCK
    scale = dq ** -0.5
    slopes = jnp.asarray([2.0 ** (-8.0 * (i + 1) / h) for i in range(h)], dtype=jnp.float32)
    kpos = jnp.arange(s)
    qb = jnp.moveaxis(q.reshape(b, nblk, DA_BLOCK, h, 2, dq), 1, 0)

    def block(args):
        q_blk, i = args
        qpos = i * DA_BLOCK + jnp.arange(DA_BLOCK)
        dist = jnp.abs(qpos[:, None] - kpos[None, :]).astype(jnp.float32)
        alibi = -slopes[:, None, None] * dist[None]
        sc = jnp.einsum('bqhcd,bkhcd->bhcqk', q_blk, k).astype(jnp.float32) * scale + alibi[None, :, None]
        p = jax.nn.softmax(sc, axis=-1)
        a = p[:, :, 0] - lam * p[:, :, 1]
        return jnp.einsum('bhqk,bkhd->bqhd', a.astype(v.dtype), v)

    o = lax.map(block, (qb, jnp.arange(nblk)))
    o = jnp.moveaxis(o, 0, 1).reshape(b, s, h, v.shape[-1])
    return _rms(o, subln_g) * (1.0 - lam_init)


def _short_conv(x, w):
    ch = x.shape[-1]
    return lax.conv_general_dilated(
        x, w[:, None, :].astype(x.dtype), window_strides=(1,),
        padding=[(GDN_CONV // 2, GDN_CONV // 2)],
        dimension_numbers=('NWC', 'WIO', 'NWC'), feature_group_count=ch)


def _chunk_gated_delta(q, k, v, beta, g):
    b, h, s, dk = q.shape
    dv = v.shape[-1]
    C = GDN_CHUNK
    n = s // C
    q = q.reshape(b, h, n, C, dk)
    k = k.reshape(b, h, n, C, dk)
    v = v.reshape(b, h, n, C, dv)
    beta = beta.reshape(b, h, n, C)
    gc = jnp.cumsum(g.reshape(b, h, n, C), axis=-1)
    tri_incl = jnp.tril(jnp.ones((C, C), dtype=bool))
    tri_strict = jnp.tril(jnp.ones((C, C), dtype=bool), -1)
    decay = jnp.exp(jnp.where(tri_incl, gc[..., :, None] - gc[..., None, :], -jnp.inf))
    kb = k * beta[..., None]
    a = jnp.where(tri_strict, jnp.einsum('bhnid,bhnjd->bhnij', kb, k) * decay, 0.0)
    eye = jnp.eye(C, dtype=jnp.float32)
    t = lax.linalg.triangular_solve(eye + a, jnp.broadcast_to(eye, a.shape),
                                    left_side=True, lower=True, unit_diagonal=True)
    u = jnp.einsum('bhnij,bhnjd->bhnid', t, v * beta[..., None])
    w = jnp.einsum('bhnij,bhnjd->bhnid', t, kb * jnp.exp(gc)[..., None])
    qg = q * jnp.exp(gc)[..., None]
    a_intra = jnp.einsum('bhnid,bhnjd->bhnij', q, k) * decay
    k_tail = k * jnp.exp(gc[..., -1:] - gc)[..., None]
    g_last = jnp.exp(gc[..., -1])

    def step(state, inp):
        u_i, w_i, qg_i, ai_i, kt_i, gl_i = inp
        v_new = u_i - jnp.einsum('bhcd,bhde->bhce', w_i, state)
        o = jnp.einsum('bhcd,bhde->bhce', qg_i, state) + jnp.einsum('bhij,bhje->bhie', ai_i, v_new)
        state = state * gl_i[..., None, None] + jnp.einsum('bhcd,bhce->bhde', kt_i, v_new)
        return state, o

    xs = tuple(jnp.moveaxis(z, 2, 0) for z in (u, w, qg, a_intra, k_tail, g_last))
    _, o = lax.scan(step, jnp.zeros((b, h, dk, dv), jnp.float32), xs)
    return jnp.moveaxis(o, 0, 2).reshape(b, h, s, dv)


def _gated_deltanet(q, k, v, z, beta_logit, alpha_logit, conv_w, a_log, dt_bias, norm_g):
    b, s, _ = q.shape
    hh = GDN_HEADS
    qkv = jax.nn.silu(_short_conv(jnp.concatenate([q, k, v], axis=-1), conv_w))
    q, k, v = qkv[..., :GDN_K_W], qkv[..., GDN_K_W:2 * GDN_K_W], qkv[..., 2 * GDN_K_W:]
    q = (_l2norm(q.reshape(b, s, hh, GDN_DK)) * GDN_DK ** -0.5).transpose(0, 2, 1, 3)
    k = _l2norm(k.reshape(b, s, hh, GDN_DK)).transpose(0, 2, 1, 3)
    v = v.reshape(b, s, hh, GDN_DV).astype(jnp.float32).transpose(0, 2, 1, 3)
    beta = jax.nn.sigmoid(beta_logit.astype(jnp.float32).reshape(b, s, 2, hh))
    g = -jnp.exp(a_log.astype(jnp.float32)) * jax.nn.softplus(
        alpha_logit.astype(jnp.float32).reshape(b, s, 2, hh) + dt_bias.astype(jnp.float32))
    beta = beta.transpose(2, 0, 3, 1)
    g = g.transpose(2, 0, 3, 1)
    o_f = _chunk_gated_delta(q, k, v, beta[0], g[0])
    o_b = jnp.flip(_chunk_gated_delta(jnp.flip(q, 2), jnp.flip(k, 2), jnp.flip(v, 2),
                                      jnp.flip(beta[1], -1), jnp.flip(g[1], -1)), 2)
    o = (o_f + o_b).transpose(0, 2, 1, 3).astype(z.dtype)
    o = _rms(o, norm_g) * jax.nn.silu(z.reshape(b, s, hh, GDN_DV))
    return o.reshape(b, s, GDN_V_W)


def setup_inputs(seed: int = 0) -> dict:
    key = jax.random.key(seed)
    ks = jax.random.split(key, 32)
    f32 = jnp.float32
    L, D = DEPTH, D_MODEL

    def nrm(k, shape, scale):
        return jax.random.normal(k, shape, f32) * scale

    dt = jnp.exp(jax.random.uniform(ks[18], (L, 2, GDN_HEADS), f32, math.log(1e-3), math.log(1e-1)))
    return {
        "x": nrm(ks[0], (BATCH, SEQ, D), 1.0),
        "c": nrm(ks[1], (BATCH, D), 1.0),
        "ada_w": nrm(ks[2], (L, D, 6 * D), 0.5 * D ** -0.5),
        "ada_b": nrm(ks[3], (L, 6 * D), 0.02),
        "norm1_g": 1.0 + nrm(ks[4], (L, D), 0.02),
        "norm2_g": 1.0 + nrm(ks[5], (L, D), 0.02),
        "w_in": nrm(ks[6], (L, D, IN_WIDTH), D ** -0.5),
        "gate_w": nrm(ks[7], (L, D, N_BRANCHES * D), D ** -0.5),
        "gate_b": nrm(ks[8], (L, N_BRANCHES * D), 0.1),
        "na_qnorm_g": 1.0 + nrm(ks[9], (L, HEAD_DIM), 0.02),
        "na_knorm_g": 1.0 + nrm(ks[10], (L, HEAD_DIM), 0.02),
        "na_rpb": nrm(ks[11], (L, NA_HEADS, 2 * NA_WIN_R - 1, 2 * NA_WIN_C - 1), 0.1),
        "da_qnorm_g": 1.0 + nrm(ks[12], (L, DA_QK_DIM), 0.02),
        "da_knorm_g": 1.0 + nrm(ks[13], (L, DA_QK_DIM), 0.02),
        "da_lambda": nrm(ks[14], (L, 4, DA_QK_DIM), 0.1),
        "da_subln_g": 1.0 + nrm(ks[15], (L, DA_V_DIM), 0.02),
        "gdn_conv_w": nrm(ks[16], (L, GDN_CONV, GDN_CONV_CH), GDN_CONV ** -0.5),
        "gdn_a_log": jnp.log(jax.random.uniform(ks[17], (L, 2, GDN_HEADS), f32, 1.0, 16.0)),
        "gdn_dt_bias": dt + jnp.log(-jnp.expm1(-dt)),
        "gdn_norm_g": 1.0 + nrm(ks[19], (L, GDN_DV), 0.02),
        "w_branch_a": nrm(ks[20], (L, NA_W, D), NA_W ** -0.5),
        "w_branch_b": nrm(ks[21], (L, DA_V_W, D), DA_V_W ** -0.5),
        "w_branch_c": nrm(ks[22], (L, GDN_V_W, D), GDN_V_W ** -0.5),
        "w_out": nrm(ks[23], (L, D, D), D ** -0.5),
        "ffn_w1": nrm(ks[24], (L, D, FFN_HIDDEN), D ** -0.5),
        "ffn_w3": nrm(ks[25], (L, D, FFN_HIDDEN), D ** -0.5),
        "ffn_w2": nrm(ks[26], (L, FFN_HIDDEN, D), FFN_HIDDEN ** -0.5),
    }


def reference(x, c, ada_w, ada_b, norm1_g, norm2_g, w_in, gate_w, gate_b,
              na_qnorm_g, na_knorm_g, na_rpb, da_qnorm_g, da_knorm_g, da_lambda, da_subln_g,
              gdn_conv_w, gdn_a_log, gdn_dt_bias, gdn_norm_g,
              w_branch_a, w_branch_b, w_branch_c, w_out, ffn_w1, ffn_w3, ffn_w2):
    b, s, d = x.shape
    for l in range(DEPTH):
        mod = jax.nn.silu(c) @ ada_w[l] + ada_b[l]
        sh1, sc1, gt1, sh2, sc2, gt2 = jnp.split(mod[:, None, :], 6, axis=-1)
        h = _rms(x, norm1_g[l]) * (1.0 + sc1) + sh1
        (na_q, na_k, na_v, da_q, da_k, da_v,
         gd_q, gd_k, gd_v, gd_z, gd_beta, gd_alpha) = _split_cols(h @ w_in[l])

        ya = _neighbourhood_attention(
            _rms(na_q.reshape(b, s, NA_HEADS, HEAD_DIM), na_qnorm_g[l]),
            _rms(na_k.reshape(b, s, NA_HEADS, HEAD_DIM), na_knorm_g[l]),
            na_v.reshape(b, s, NA_HEADS, HEAD_DIM), na_rpb[l]).reshape(b, s, NA_W)

        lam_init = 0.8 - 0.6 * math.exp(-0.3 * l)
        lp = da_lambda[l].astype(jnp.float32)
        lam = jnp.exp(jnp.sum(lp[0] * lp[1])) - jnp.exp(jnp.sum(lp[2] * lp[3])) + lam_init
        yb = _diff_attention(
            _rms(da_q.reshape(b, s, DA_HEADS, 2, DA_QK_DIM), da_qnorm_g[l]),
            _rms(da_k.reshape(b, s, DA_HEADS, 2, DA_QK_DIM), da_knorm_g[l]),
            da_v.reshape(b, s, DA_HEADS, DA_V_DIM), lam, lam_init, da_subln_g[l]).reshape(b, s, DA_V_W)

        yc = _gated_deltanet(gd_q, gd_k, gd_v, gd_z, gd_beta, gd_alpha,
                             gdn_conv_w[l], gdn_a_log[l], gdn_dt_bias[l], gdn_norm_g[l])

        gates = jax.nn.sigmoid(h @ gate_w[l] + gate_b[l]).reshape(b, s, N_BRANCHES, d)
        merged = (gates[:, :, 0] * (ya @ w_branch_a[l])
                  + gates[:, :, 1] * (yb @ w_branch_b[l])
                  + gates[:, :, 2] * (yc @ w_branch_c[l]))
        x = x + gt1 * (merged @ w_out[l])

        h2 = _rms(x, norm2_g[l]) * (1.0 + sc2) + sh2
        x = x + gt2 * ((jax.nn.silu(h2 @ ffn_w1[l]) * (h2 @ ffn_w3[l])) @ ffn_w2[l])
    return x
```

```python
import functools
import math

import jax
import jax.numpy as jnp
from jax import lax
from jax.experimental import pallas as pl
from jax.experimental.pallas import tpu as pltpu

F32 = jnp.float32
BF16 = jnp.bfloat16

GRID_W = 64
HEAD_DIM = 128
NA_HEADS = 4
NA_WIN_R = 8
NA_WIN_C = 16
DA_HEADS = 4
DA_QK_DIM = 64
DA_V_DIM = 128
GDN_HEADS = 8
GDN_DK = 128
GDN_DV = 128
GDN_CONV = 5
GDN_CHUNK = 64
GDN_SUPER = 256
N_BRANCHES = 3
RMS_EPS = 1e-6

NA_W = NA_HEADS * HEAD_DIM
DA_QK_W = DA_HEADS * 2 * DA_QK_DIM
DA_V_W = DA_HEADS * DA_V_DIM
GDN_K_W = GDN_HEADS * GDN_DK
GDN_V_W = GDN_HEADS * GDN_DV

LANES = 128
VMEM_LIMIT = 56 * 1024 * 1024
NEG_BIG = -1e30

_LOG2_LANES = LANES.bit_length() - 1
_LOG2_CHUNK = GDN_CHUNK.bit_length() - 1

_NT = (((1,), (1,)), ((), ()))
_TN = (((0,), (0,)), ((), ()))


def _cparams(sem):
    return pltpu.CompilerParams(dimension_semantics=sem, vmem_limit_bytes=VMEM_LIMIT)


def _sigmoid(x):
    return 1.0 / (1.0 + jnp.exp(-x))


def _dot(a, b):
    return jnp.dot(a.astype(BF16), b.astype(BF16), preferred_element_type=F32)


def _dot_nt(a, b):
    return lax.dot_general(a.astype(BF16), b.astype(BF16), _NT, preferred_element_type=F32)


def _mod_kernel(c_ref, w_ref, b_ref, o_ref):
    c = c_ref[...]
    a = c * _sigmoid(c)
    o_ref[0] = jnp.dot(a, w_ref[0], preferred_element_type=F32,
                       precision=lax.Precision.HIGHEST) + b_ref[0]


def _adaln(c, ada_w, ada_b):
    depth, d, n = ada_w.shape
    b = c.shape[0]
    rows = 8
    cp = jnp.zeros((rows, d), F32).at[:b].set(c)
    tn = 1024
    out = pl.pallas_call(
        _mod_kernel,
        grid=(depth, n // tn),
        in_specs=[pl.BlockSpec((rows, d), lambda l, j: (0, 0)),
                  pl.BlockSpec((1, d, tn), lambda l, j: (l, 0, j)),
                  pl.BlockSpec((1, 1, tn), lambda l, j: (l, 0, j))],
        out_specs=pl.BlockSpec((1, rows, tn), lambda l, j: (l, 0, j)),
        out_shape=jax.ShapeDtypeStruct((depth, rows, n), F32),
        compiler_params=_cparams(("arbitrary", "arbitrary")),
        name="adaln_mod",
    )(cp, ada_w, ada_b.reshape(depth, 1, n))
    return out[:, :b]


def _norm_kernel(x_ref, g_ref, sc_ref, sh_ref, o_ref):
    x = x_ref[0]
    y = x * lax.rsqrt(jnp.mean(x * x, axis=-1, keepdims=True) + RMS_EPS) * g_ref[...]
    o_ref[0] = (y * (1.0 + sc_ref[0]) + sh_ref[0]).astype(o_ref.dtype)


def _norm_mod(x, g, sc, sh):
    b, s, d = x.shape
    ts = min(512, s)
    return pl.pallas_call(
        _norm_kernel,
        grid=(b, s // ts),
        in_specs=[pl.BlockSpec((1, ts, d), lambda i, j: (i, j, 0)),
                  pl.BlockSpec((1, d), lambda i, j: (0, 0)),
                  pl.BlockSpec((1, 1, d), lambda i, j: (i, 0, 0)),
                  pl.BlockSpec((1, 1, d), lambda i, j: (i, 0, 0))],
        out_specs=pl.BlockSpec((1, ts, d), lambda i, j: (i, j, 0)),
        out_shape=jax.ShapeDtypeStruct((b, s, d), BF16),
        compiler_params=_cparams(("parallel", "parallel")),
        name="norm_mod",
    )(x, g.reshape(1, d), sc.reshape(b, 1, d), sh.reshape(b, 1, d))


def _proj_kernel(h_ref, w_ref, *rest, group):
    acc = jnp.dot(h_ref[...], w_ref[...], preferred_element_type=F32)
    if group is None:
        (o_ref,) = rest
        o_ref[...] = acc.astype(o_ref.dtype)
        return
    gain_ref, o_ref = rest
    tn = acc.shape[1]
    for s in range(tn // LANES):
        sl = slice(s * LANES, (s + 1) * LANES)
        a = acc[:, sl]
        sq = a * a
        if group == LANES:
            ms = jnp.sum(sq, axis=-1, keepdims=True) * (1.0 / LANES)
        else:
            lo = lax.broadcasted_iota(jnp.int32, a.shape, 1) < group
            s_lo = jnp.sum(jnp.where(lo, sq, 0.0), axis=-1, keepdims=True)
            s_hi = jnp.sum(jnp.where(lo, 0.0, sq), axis=-1, keepdims=True)
            ms = jnp.where(lo, s_lo, s_hi) * (1.0 / group)
        y = a * lax.rsqrt(ms + RMS_EPS) * gain_ref[:, sl]
        o_ref[:, sl] = y.astype(o_ref.dtype)


def _proj(h2d, w, out_dtype, group=None, gain=None):
    t, d = h2d.shape
    n = w.shape[1]
    tm = min(1024, t)
    tn = min(512, n)
    in_specs = [pl.BlockSpec((tm, d), lambda i, j: (i, 0)),
                pl.BlockSpec((d, tn), lambda i, j: (0, j))]
    args = [h2d, w.astype(BF16)]
    if group is not None:
        assert group in (LANES // 2, LANES)
        in_specs.append(pl.BlockSpec((1, tn), lambda i, j: (0, j)))
        args.append(gain.reshape(1, n).astype(F32))
    return pl.pallas_call(
        functools.partial(_proj_kernel, group=group),
        grid=(t // tm, n // tn),
        in_specs=in_specs,
        out_specs=pl.BlockSpec((tm, tn), lambda i, j: (i, j)),
        out_shape=jax.ShapeDtypeStruct((t, n), out_dtype),
        compiler_params=_cparams(("parallel", "arbitrary")),
        name="proj",
    )(*args)


NA_ROWS_PER_STEP = 8
NA_BAND = NA_WIN_R * GRID_W


def _na_bias_table(rpb):
    h = rpb.shape[0]
    w = jnp.arange(GRID_W)
    col_start = jnp.clip(w - NA_WIN_C // 2, 0, GRID_W - NA_WIN_C)
    ck = jnp.arange(GRID_W)
    inwin = (ck[None, :] >= col_start[:, None]) & (ck[None, :] < col_start[:, None] + NA_WIN_C)
    cbi = jnp.clip(ck[None, :] - w[:, None] + (NA_WIN_C - 1), 0, 2 * NA_WIN_C - 2)
    e = jnp.arange(NA_WIN_R)
    j = jnp.arange(NA_WIN_R)
    rbi = j[None, :] + (NA_WIN_R - 1) - e[:, None]
    tab = rpb[:, rbi]
    tab = tab[:, :, :, cbi]
    tab = jnp.where(inwin[None, None, None], tab, NEG_BIG)
    return tab.transpose(1, 0, 3, 2, 4).reshape(NA_WIN_R, h, GRID_W, NA_BAND).astype(F32)


def _na_kernel(q_ref, kp_ref, kc_ref, kn_ref, vp_ref, vc_ref, vn_ref, tab_ref, o_ref,
               kbuf, vbuf, *, rows):
    rb = pl.program_id(1)
    blk = NA_ROWS_PER_STEP * GRID_W
    for n, (kr, vr) in enumerate(((kp_ref, vp_ref), (kc_ref, vc_ref), (kn_ref, vn_ref))):
        kbuf[n * blk:(n + 1) * blk, :] = kr[0]
        vbuf[n * blk:(n + 1) * blk, :] = vr[0]
    for i in range(NA_ROWS_PER_STEP):
        r = rb * NA_ROWS_PER_STEP + i
        r_start = jnp.clip(r - NA_WIN_R // 2, 0, rows - NA_WIN_R)
        e = r - r_start
        loc = pl.multiple_of((r_start - (rb - 1) * NA_ROWS_PER_STEP) * GRID_W, GRID_W)
        rs = slice(i * GRID_W, (i + 1) * GRID_W)
        for h in range(NA_HEADS):
            hs = slice(h * HEAD_DIM, (h + 1) * HEAD_DIM)
            q = q_ref[0, rs, hs]
            kb = kbuf[pl.ds(loc, NA_BAND), hs]
            vb = vbuf[pl.ds(loc, NA_BAND), hs]
            s = lax.dot_general(q, kb, _NT, preferred_element_type=F32) + tab_ref[e, h]
            m = jnp.max(s, axis=-1, keepdims=True)
            p = jnp.exp(s - m)
            l = jnp.sum(p, axis=-1, keepdims=True)
            o = jnp.dot(p.astype(BF16), vb, preferred_element_type=F32) / l
            o_ref[0, rs, hs] = o.astype(o_ref.dtype)


def _natten(qk, v, rpb):
    b, s, _ = v.shape
    rows = s // GRID_W
    blk = NA_ROWS_PER_STEP * GRID_W
    nrb = s // blk
    tab = _na_bias_table(rpb)

    def at(off, col):
        return lambda i, j: (i, jnp.clip(j + off, 0, nrb - 1), col)

    return pl.pallas_call(
        functools.partial(_na_kernel, rows=rows),
        grid=(b, nrb),
        in_specs=[pl.BlockSpec((1, blk, NA_W), at(0, 0)),
                  pl.BlockSpec((1, blk, NA_W), at(-1, 1)),
                  pl.BlockSpec((1, blk, NA_W), at(0, 1)),
                  pl.BlockSpec((1, blk, NA_W), at(1, 1)),
                  pl.BlockSpec((1, blk, NA_W), at(-1, 0)),
                  pl.BlockSpec((1, blk, NA_W), at(0, 0)),
                  pl.BlockSpec((1, blk, NA_W), at(1, 0)),
                  pl.BlockSpec(tab.shape, lambda i, j: (0, 0, 0, 0))],
        out_specs=pl.BlockSpec((1, blk, NA_W), at(0, 0)),
        out_shape=jax.ShapeDtypeStruct((b, s, NA_W), BF16),
        scratch_shapes=[pltpu.VMEM((3 * blk, NA_W), BF16), pltpu.VMEM((3 * blk, NA_W), BF16)],
        compiler_params=_cparams(("parallel", "parallel")),
        name="natten",
    )(qk, qk, qk, qk, v, v, v, tab)


DA_TILE = 512
DA_POS_SPLIT = 16


def _da_slopes():
    return [2.0 ** (-8.0 * (i + 1) / DA_HEADS) for i in range(DA_HEADS)]


def _da_pos_tables(tile):
    slopes = jnp.asarray(_da_slopes(), F32)[:, None]
    pos = jnp.arange(tile)
    hi = (pos - pos % DA_POS_SPLIT).astype(F32)[None, :]
    lo = (pos % DA_POS_SPLIT).astype(F32)[None, :]
    one = jnp.ones((DA_HEADS, tile), F32)
    kcols = jnp.stack([slopes * hi, slopes * lo, one, one], axis=-1)
    qcols = jnp.stack([one, one, -slopes * hi, -slopes * lo], axis=-1)
    half = LANES // 2

    def lanes(cols):
        z = jnp.zeros((DA_HEADS, tile, half - 4), F32)
        blk = jnp.concatenate([cols, z], axis=-1)
        return jnp.concatenate([blk, blk], axis=-1).astype(BF16)

    dist = jnp.abs(pos[:, None] - pos[None, :]).astype(F32)
    bdiag = -slopes[:, :, None] * dist[None]
    lane = jnp.arange(LANES)[None, :]
    low_mask = jnp.broadcast_to(lane < half, (tile, LANES))
    ones_col = jnp.broadcast_to(lane == 0, (tile, LANES))
    aux = jnp.stack([low_mask, ones_col]).astype(BF16)
    return lanes(qcols), lanes(kcols), aux, bdiag


def _da_kernel(q_ref, k_ref, v_ref, qpos_ref, kpos_ref, aux_ref, bdiag_ref, slope_ref, lam_ref,
               g_ref, o_ref, qa_sc, m_sc, acc_sc, *, tile, lam_init):
    hd = pl.program_id(1)
    qi = pl.program_id(2)
    ki = pl.program_id(3)
    nk = pl.num_programs(3)
    low = aux_ref[0] > 0.5
    own = (low, jnp.logical_not(low))

    @pl.when(ki == 0)
    def _():
        m_sc[...] = jnp.full(m_sc.shape, NEG_BIG, F32)
        acc_sc[...] = jnp.zeros(acc_sc.shape, F32)
        q = q_ref[0]
        qpos = qpos_ref[0]
        zero = jnp.zeros_like(q)
        for c in range(2):
            qa_sc[0, c] = jnp.where(own[c], q, qpos)
            qa_sc[1, c] = jnp.where(own[c], q, zero)
            qa_sc[2, c] = jnp.where(own[c], q, -qpos)

    k = k_ref[0]
    kpos = kpos_ref[0]
    va = jnp.concatenate([v_ref[0], aux_ref[1]], axis=1)
    slope = slope_ref[hd]
    shift = -slope * (tile * jnp.abs(ki - qi)).astype(F32)

    def tile_update(sel, diag):
        for c in range(2):
            ka = jnp.where(own[c], k, kpos)
            s = lax.dot_general(qa_sc[sel, c], ka, _NT, preferred_element_type=F32)
            if diag:
                s = s + bdiag_ref[0]
            m_old = m_sc[c]
            m_new = jnp.maximum(m_old, jnp.max(s, axis=-1, keepdims=True) + shift)
            alpha = jnp.exp(m_old - m_new)
            p = jnp.exp(s - (m_new - shift))
            acc_sc[c] = alpha * acc_sc[c] + jnp.dot(p.astype(BF16), va, preferred_element_type=F32)
            m_sc[c] = m_new

    @pl.when(ki < qi)
    def _():
        tile_update(0, False)

    @pl.when(ki == qi)
    def _():
        tile_update(1, True)

    @pl.when(ki > qi)
    def _():
        tile_update(2, False)

    @pl.when(ki == nk - 1)
    def _():
        lp = lam_ref[...]
        lam = (jnp.exp(jnp.sum(lp[0:1] * lp[1:2], axis=-1, keepdims=True))
               - jnp.exp(jnp.sum(lp[2:3] * lp[3:4], axis=-1, keepdims=True)) + lam_init)
        a0 = acc_sc[0]
        a1 = acc_sc[1]
        o0 = a0[:, :DA_V_DIM] / a0[:, DA_V_DIM:DA_V_DIM + 1]
        o1 = a1[:, :DA_V_DIM] / a1[:, DA_V_DIM:DA_V_DIM + 1]
        a = o0 - lam * o1
        y = a * lax.rsqrt(jnp.mean(a * a, axis=-1, keepdims=True) + RMS_EPS) * g_ref[...]
        o_ref[0] = (y * (1.0 - lam_init)).astype(o_ref.dtype)


def _diff_attn(qk, v, lam_p, subln_g, lam_init):
    b, s, _ = v.shape
    tile = min(DA_TILE, s)
    nt = s // tile
    qpos, kpos, aux, bdiag = _da_pos_tables(tile)
    slopes = jnp.asarray(_da_slopes(), F32)
    return pl.pallas_call(
        functools.partial(_da_kernel, tile=tile, lam_init=lam_init),
        grid=(b, DA_HEADS, nt, nt),
        in_specs=[pl.BlockSpec((1, tile, LANES), lambda i, h, q, k: (i, q, h)),
                  pl.BlockSpec((1, tile, LANES), lambda i, h, q, k: (i, k, DA_HEADS + h)),
                  pl.BlockSpec((1, tile, DA_V_DIM), lambda i, h, q, k: (i, k, h)),
                  pl.BlockSpec((1, tile, LANES), lambda i, h, q, k: (h, 0, 0)),
                  pl.BlockSpec((1, tile, LANES), lambda i, h, q, k: (h, 0, 0)),
                  pl.BlockSpec((2, tile, LANES), lambda i, h, q, k: (0, 0, 0)),
                  pl.BlockSpec((1, tile, tile), lambda i, h, q, k: (h, 0, 0)),
                  pl.BlockSpec(memory_space=pltpu.SMEM),
                  pl.BlockSpec((4, DA_QK_DIM), lambda i, h, q, k: (0, 0)),
                  pl.BlockSpec((1, DA_V_DIM), lambda i, h, q, k: (0, 0))],
        out_specs=pl.BlockSpec((1, tile, DA_V_DIM), lambda i, h, q, k: (i, q, h)),
        out_shape=jax.ShapeDtypeStruct((b, s, DA_V_W), BF16),
        scratch_shapes=[pltpu.VMEM((3, 2, tile, LANES), BF16),
                        pltpu.VMEM((2, tile, 1), F32),
                        pltpu.VMEM((2, tile, 2 * DA_V_DIM), F32)],
        compiler_params=_cparams(("parallel", "parallel", "parallel", "arbitrary")),
        name="diff_attn",
    )(qk, qk, v, qpos, kpos, aux, bdiag, slopes, lam_p.astype(F32), subln_g.reshape(1, DA_V_DIM).astype(F32))


def _gdn_gate_kernel(lg_ref, alog_ref, dt_ref, beta_ref, gc_ref, gt_ref):
    lg = lg_ref[0]
    nh = 2 * GDN_HEADS
    beta_ref[0] = _sigmoid(lg[0:nh])
    z = lg[nh:2 * nh] + dt_ref[...]
    sp = jnp.maximum(z, 0.0) + jnp.log(1.0 + jnp.exp(-jnp.abs(z)))
    g = -jnp.exp(alog_ref[...]) * sp
    ts = g.shape[1]
    pos = lax.broadcasted_iota(jnp.int32, g.shape, 1) & (GDN_CHUNK - 1)
    pre = g
    suf = g
    sh = 1
    while sh < GDN_CHUNK:
        pre = pre + jnp.where(pos >= sh, pltpu.roll(pre, sh, axis=1), 0.0)
        suf = suf + jnp.where(pos < GDN_CHUNK - sh, pltpu.roll(suf, ts - sh, axis=1), 0.0)
        sh *= 2
    bwd = lax.broadcasted_iota(jnp.int32, g.shape, 0) >= GDN_HEADS
    gc_ref[0] = jnp.where(bwd, suf, pre)
    gt_ref[0] = jnp.where(bwd, pre, suf) - g


def _gdn_gates(lgt, a_log, dt_bias):
    b, _, s = lgt.shape
    nh = 2 * GDN_HEADS
    ts = min(2048, s)
    shp = jax.ShapeDtypeStruct((b, nh, s), F32)
    spec = pl.BlockSpec((1, nh, ts), lambda i, j: (i, 0, j))
    return pl.pallas_call(
        _gdn_gate_kernel,
        grid=(b, s // ts),
        in_specs=[pl.BlockSpec((1, 2 * nh, ts), lambda i, j: (i, 0, j)),
                  pl.BlockSpec((nh, 1), lambda i, j: (0, 0)),
                  pl.BlockSpec((nh, 1), lambda i, j: (0, 0))],
        out_specs=[spec, spec, spec],
        out_shape=[shp, shp, shp],
        compiler_params=_cparams(("parallel", "parallel")),
        name="gdn_gates",
    )(lgt, a_log.reshape(nh, 1).astype(F32), dt_bias.reshape(nh, 1).astype(F32))


GDN_HALO = 8


def _gdn_conv_kernel(x_ref, p_ref, n_ref, w_ref, o_ref, ext, *, ts):
    cg = pl.program_id(0)
    i = pl.program_id(2)
    last = pl.num_programs(2) - 1
    ext[0:GDN_HALO, :] = jnp.where(i > 0, p_ref[0], 0.0)
    ext[GDN_HALO:GDN_HALO + ts, :] = x_ref[0]
    ext[GDN_HALO + ts:2 * GDN_HALO + ts, :] = jnp.where(i < last, n_ref[0], 0.0)
    y = None
    for tap in range(GDN_CONV):
        off = GDN_HALO + tap - GDN_CONV // 2
        term = ext[off:off + ts, :] * w_ref[0, tap:tap + 1, :]
        y = term if y is None else y + term
    y = y * _sigmoid(y)
    qscale = jnp.where(cg == 0, GDN_DK ** -0.5, 1.0).astype(F32)
    is_v = cg == 2
    for h in range(GDN_HEADS):
        hs = slice(h * GDN_DK, (h + 1) * GDN_DK)
        a = y[:, hs]
        nrm = lax.rsqrt(jnp.sum(a * a, axis=-1, keepdims=True) + RMS_EPS) * qscale
        o_ref[0, 0, :, hs] = a * jnp.where(is_v, 1.0, nrm)


def _gdn_conv(u, conv_w):
    b, s, _ = u.shape
    cw = GDN_K_W
    ts = min(512, s)
    nb = s // GDN_HALO
    per = ts // GDN_HALO
    w3 = conv_w.reshape(GDN_CONV, 3, cw).transpose(1, 0, 2)
    w3 = jnp.pad(w3, ((0, 0), (0, 8 - GDN_CONV), (0, 0))).astype(F32)
    return pl.pallas_call(
        functools.partial(_gdn_conv_kernel, ts=ts),
        grid=(3, b, s // ts),
        in_specs=[pl.BlockSpec((1, ts, cw), lambda c, i, j: (i, j, c)),
                  pl.BlockSpec((1, GDN_HALO, cw),
                               lambda c, i, j: (i, jnp.maximum(j * per - 1, 0), c)),
                  pl.BlockSpec((1, GDN_HALO, cw),
                               lambda c, i, j: (i, jnp.minimum((j + 1) * per, nb - 1), c)),
                  pl.BlockSpec((1, 8, cw), lambda c, i, j: (c, 0, 0))],
        out_specs=pl.BlockSpec((1, 1, ts, cw), lambda c, i, j: (c, i, j, 0)),
        out_shape=jax.ShapeDtypeStruct((3, b, s, cw), F32),
        scratch_shapes=[pltpu.VMEM((ts + 2 * GDN_HALO, cw), F32)],
        compiler_params=_cparams(("parallel", "parallel", "parallel")),
        name="gdn_conv",
    )(u, u, u, w3)


GDN_INV_PASSES = 3


def _split_dot(a, b, passes):
    a_hi = a.astype(BF16)
    b_hi = b.astype(BF16)
    out = jnp.dot(a_hi, b_hi, preferred_element_type=F32)
    if passes >= 2:
        b_lo = (b - b_hi.astype(F32)).astype(BF16)
        out = out + jnp.dot(a_hi, b_lo, preferred_element_type=F32)
    if passes >= 3:
        a_lo = (a - a_hi.astype(F32)).astype(BF16)
        out = out + jnp.dot(a_lo, b_hi, preferred_element_type=F32)
    return out


def _gdn_scan_kernel(q_ref, k_ref, v_ref, cols_ref, rows_ref, o_ref, state, *, rev):
    hd = pl.program_id(1)
    n = pl.program_id(2)
    sc = GDN_SUPER
    nh = GDN_HEADS

    @pl.when(n == 0)
    def _():
        state[...] = jnp.zeros(state.shape, F32)

    q = q_ref[0, 0]
    k = k_ref[0, 0]
    v = v_ref[0, 0]

    cols = cols_ref[0, 0]
    ri = lax.broadcasted_iota(jnp.int32, (LANES, 3 * LANES), 0)
    ci = lax.broadcasted_iota(jnp.int32, (LANES, 3 * LANES), 1)
    onehot = jnp.where(ri == hd + nh * (ci >> _LOG2_LANES), 1.0, 0.0).astype(F32)
    bc = jnp.dot(cols, onehot, preferred_element_type=F32, precision=lax.Precision.HIGHEST)
    beta = bc[:, 0:LANES]
    gc = bc[:, LANES:2 * LANES]
    gt = bc[:, 2 * LANES:3 * LANES]
    gr = rows_ref[0, 0, pl.ds(hd, 1), :]

    ii = lax.broadcasted_iota(jnp.int32, (sc, sc), 0)
    jj = lax.broadcasted_iota(jnp.int32, (sc, sc), 1)
    same = (ii >> _LOG2_CHUNK) == (jj >> _LOG2_CHUNK)
    if rev:
        incl = same & (ii <= jj)
        strict = same & (ii < jj)
    else:
        incl = same & (ii >= jj)
        strict = same & (ii > jj)

    gc2 = jnp.concatenate([gc, gc], axis=1)
    decay = jnp.exp(jnp.where(incl, gc2 - gr, NEG_BIG))
    kb = k * beta
    a_mat = jnp.where(strict, _dot_nt(kb, k) * decay, 0.0)

    x = -a_mat
    t_mat = jnp.where(ii == jj, 1.0, 0.0).astype(F32) + x
    span = 2
    while span < GDN_CHUNK:
        x = _split_dot(x, x, GDN_INV_PASSES)
        t_mat = t_mat + _split_dot(t_mat, x, GDN_INV_PASSES)
        span *= 2

    egc = jnp.exp(gc)
    u = _dot(t_mat, v * beta)
    w = _dot(t_mat, kb * egc)
    qg = q * egc
    ai = _dot_nt(q, k) * decay
    kt = k * jnp.exp(gt)
    gl = jnp.exp(gc + gt)

    s_mat = state[...]
    outs = [None] * (sc // GDN_CHUNK)
    order = range(sc // GDN_CHUNK)
    for j in (reversed(order) if rev else order):
        sl = slice(j * GDN_CHUNK, (j + 1) * GDN_CHUNK)
        v_new = u[sl] - _dot(w[sl], s_mat)
        outs[j] = _dot(qg[sl], s_mat) + _dot(ai[sl, sl], v_new)
        s_mat = s_mat * gl[j * GDN_CHUNK:j * GDN_CHUNK + 1, :] + lax.dot_general(
            kt[sl].astype(BF16), v_new.astype(BF16), _TN, preferred_element_type=F32)
    state[...] = s_mat
    o_ref[0] = jnp.concatenate(outs, axis=0)


def _gdn_scan(qkv, cols, rows, rev):
    _, b, s, _ = qkv.shape
    sc = GDN_SUPER
    nsc = s // sc
    d = 1 if rev else 0

    def seq(n):
        return (nsc - 1 - n) if rev else n

    return pl.pallas_call(
        functools.partial(_gdn_scan_kernel, rev=rev),
        grid=(b, GDN_HEADS, nsc),
        in_specs=[pl.BlockSpec((1, 1, sc, GDN_DK), lambda i, h, n: (0, i, seq(n), h)),
                  pl.BlockSpec((1, 1, sc, GDN_DK), lambda i, h, n: (1, i, seq(n), h)),
                  pl.BlockSpec((1, 1, sc, GDN_DV), lambda i, h, n: (2, i, seq(n), h)),
                  pl.BlockSpec((1, 1, sc, LANES), lambda i, h, n: (i, d, seq(n), 0)),
                  pl.BlockSpec((1, 1, GDN_HEADS, sc), lambda i, h, n: (i, d, 0, seq(n)))],
        out_specs=pl.BlockSpec((1, sc, GDN_DV), lambda i, h, n: (i, seq(n), h)),
        out_shape=jax.ShapeDtypeStruct((b, s, GDN_V_W), F32),
        scratch_shapes=[pltpu.VMEM((GDN_DK, GDN_DV), F32)],
        compiler_params=_cparams(("parallel", "parallel", "arbitrary")),
        name="gdn_scan_bwd" if rev else "gdn_scan_fwd",
    )(qkv, qkv, qkv, cols, rows)


def _gdn_out_kernel(of_ref, ob_ref, z_ref, g_ref, o_ref):
    o = of_ref[0] + ob_ref[0]
    z = z_ref[0].astype(F32)
    gate = z * _sigmoid(z)
    for h in range(GDN_HEADS):
        hs = slice(h * GDN_DV, (h + 1) * GDN_DV)
        a = o[:, hs]
        y = a * lax.rsqrt(jnp.mean(a * a, axis=-1, keepdims=True) + RMS_EPS) * g_ref[...]
        o_ref[0, :, hs] = (y * gate[:, hs]).astype(o_ref.dtype)


def _gdn_out(o_f, o_b, z, norm_g):
    b, s, w = o_f.shape
    ts = min(512, s)
    spec = pl.BlockSpec((1, ts, w), lambda i, j: (i, j, 0))
    return pl.pallas_call(
        _gdn_out_kernel,
        grid=(b, s // ts),
        in_specs=[spec, spec, spec, pl.BlockSpec((1, GDN_DV), lambda i, j: (0, 0))],
        out_specs=spec,
        out_shape=jax.ShapeDtypeStruct((b, s, w), BF16),
        compiler_params=_cparams(("parallel", "parallel")),
        name="gdn_out",
    )(o_f, o_b, z, norm_g.reshape(1, GDN_DV).astype(F32))


def _gated_deltanet(u_qkv, z, lg, conv_w, a_log, dt_bias, norm_g):
    b, s, _ = u_qkv.shape
    nh = GDN_HEADS
    lgt = lg[:, :, :4 * nh].transpose(0, 2, 1)
    beta, gc, gt = _gdn_gates(lgt, a_log, dt_bias)
    rows = gc.reshape(b, 2, nh, s)
    cols = jnp.stack([beta, gc, gt], axis=1).reshape(b, 3, 2, nh, s)
    cols = cols.transpose(0, 2, 4, 1, 3).reshape(b, 2, s, 3 * nh)
    cols = jnp.pad(cols, ((0, 0), (0, 0), (0, 0), (0, LANES - 3 * nh)))
    qkv = _gdn_conv(u_qkv, conv_w)
    o_f = _gdn_scan(qkv, cols, rows, rev=False)
    o_b = _gdn_scan(qkv, cols, rows, rev=True)
    return _gdn_out(o_f, o_b, z, norm_g)


def _merge_kernel(h_ref, ya_ref, yb_ref, yc_ref, gw0_ref, gw1_ref, gw2_ref,
                  gb0_ref, gb1_ref, gb2_ref, wa_ref, wb_ref, wc_ref, wo_ref,
                  x_ref, gt_ref, o_ref, acc):
    j = pl.program_id(1)
    h = h_ref[...]

    def gate(gw_ref, gb_ref):
        return _sigmoid(jnp.dot(h, gw_ref[...], preferred_element_type=F32) + gb_ref[...])

    m = (gate(gw0_ref, gb0_ref) * jnp.dot(ya_ref[...], wa_ref[...], preferred_element_type=F32)
         + gate(gw1_ref, gb1_ref) * jnp.dot(yb_ref[...], wb_ref[...], preferred_element_type=F32)
         + gate(gw2_ref, gb2_ref) * jnp.dot(yc_ref[...], wc_ref[...], preferred_element_type=F32))
    part = jnp.dot(m.astype(BF16), wo_ref[...], preferred_element_type=F32)

    @pl.when(j == 0)
    def _():
        acc[...] = part

    @pl.when(j > 0)
    def _():
        acc[...] += part

    @pl.when(j == pl.num_programs(1) - 1)
    def _():
        o_ref[...] = x_ref[...] + gt_ref[0] * acc[...]


def _merge(h2d, ya, yb, yc, gate_w, gate_b, wa, wb, wc, w_out, x2d, gt, seq):
    t, d = x2d.shape
    tm = min(512, seq)
    tn = 512
    nj = d // tn
    per_batch = seq // tm
    gw = gate_w.astype(BF16)
    gb = gate_b.reshape(1, N_BRANCHES * d).astype(F32)
    row = lambda w: pl.BlockSpec((tm, w), lambda i, j: (i, 0))
    gws = [pl.BlockSpec((d, tn), (lambda i, j, r=r: (0, r * nj + j))) for r in range(N_BRANCHES)]
    gbs = [pl.BlockSpec((1, tn), (lambda i, j, r=r: (0, r * nj + j))) for r in range(N_BRANCHES)]
    return pl.pallas_call(
        _merge_kernel,
        grid=(t // tm, nj),
        in_specs=[row(d), row(ya.shape[1]), row(yb.shape[1]), row(yc.shape[1]),
                  *gws, *gbs,
                  pl.BlockSpec((wa.shape[0], tn), lambda i, j: (0, j)),
                  pl.BlockSpec((wb.shape[0], tn), lambda i, j: (0, j)),
                  pl.BlockSpec((wc.shape[0], tn), lambda i, j: (0, j)),
                  pl.BlockSpec((tn, d), lambda i, j: (j, 0)),
                  row(d),
                  pl.BlockSpec((1, 1, d), lambda i, j: (i // per_batch, 0, 0))],
        out_specs=row(d),
        out_shape=jax.ShapeDtypeStruct((t, d), F32),
        scratch_shapes=[pltpu.VMEM((tm, d), F32)],
        compiler_params=_cparams(("parallel", "arbitrary")),
        name="merge",
    )(h2d, ya, yb, yc, gw, gw, gw, gb, gb, gb,
      wa.astype(BF16), wb.astype(BF16), wc.astype(BF16), w_out.astype(BF16), x2d, gt)


def _ffn_kernel(h_ref, w1_ref, w3_ref, w2_ref, x_ref, gt_ref, o_ref, acc):
    j = pl.program_id(1)
    h = h_ref[...]
    a = jnp.dot(h, w1_ref[...], preferred_element_type=F32)
    b = jnp.dot(h, w3_ref[...], preferred_element_type=F32)
    part = jnp.dot((a * _sigmoid(a) * b).astype(BF16), w2_ref[...], preferred_element_type=F32)

    @pl.when(j == 0)
    def _():
        acc[...] = part

    @pl.when(j > 0)
    def _():
        acc[...] += part

    @pl.when(j == pl.num_programs(1) - 1)
    def _():
        o_ref[...] = x_ref[...] + gt_ref[0] * acc[...]


def _ffn(h2d, w1, w3, w2, x2d, gt, seq):
    t, d = x2d.shape
    f = w1.shape[1]
    tm = min(512, seq)
    tf = 512
    per_batch = seq // tm
    row = pl.BlockSpec((tm, d), lambda i, j: (i, 0))
    return pl.pallas_call(
        _ffn_kernel,
        grid=(t // tm, f // tf),
        in_specs=[row,
                  pl.BlockSpec((d, tf), lambda i, j: (0, j)),
                  pl.BlockSpec((d, tf), lambda i, j: (0, j)),
                  pl.BlockSpec((tf, d), lambda i, j: (j, 0)),
                  row,
                  pl.BlockSpec((1, 1, d), lambda i, j: (i // per_batch, 0, 0))],
        out_specs=row,
        out_shape=jax.ShapeDtypeStruct((t, d), F32),
        scratch_shapes=[pltpu.VMEM((tm, d), F32)],
        compiler_params=_cparams(("parallel", "arbitrary")),
        name="ffn",
    )(h2d, w1.astype(BF16), w3.astype(BF16), w2.astype(BF16), x2d, gt)


def _layer(l, x, mod, norm1_g, norm2_g, w_in, gate_w, gate_b, na_qnorm_g, na_knorm_g, na_rpb,
           da_qnorm_g, da_knorm_g, da_lambda, da_subln_g, gdn_conv_w, gdn_a_log, gdn_dt_bias,
           gdn_norm_g, w_branch_a, w_branch_b, w_branch_c, w_out, ffn_w1, ffn_w3, ffn_w2):
    b, s, d = x.shape
    t = b * s
    sh1, sc1, gt1, sh2, sc2, gt2 = [mod[:, i * d:(i + 1) * d] for i in range(6)]

    h = _norm_mod(x, norm1_g, sc1, sh1)
    h2d = h.reshape(t, d)

    c0 = 0
    c1 = c0 + 2 * NA_W
    c2 = c1 + NA_W
    c3 = c2 + 2 * DA_QK_W
    c4 = c3 + DA_V_W
    c5 = c4 + 2 * GDN_K_W + GDN_V_W
    c6 = c5 + GDN_V_W
    c7 = c6 + 4 * GDN_HEADS

    na_gain = jnp.concatenate([jnp.tile(na_qnorm_g * HEAD_DIM ** -0.5, NA_HEADS),
                               jnp.tile(na_knorm_g, NA_HEADS)])
    na_qk = _proj(h2d, w_in[:, c0:c1], BF16, group=HEAD_DIM, gain=na_gain).reshape(b, s, -1)
    na_v = _proj(h2d, w_in[:, c1:c2], BF16).reshape(b, s, -1)
    da_gain = jnp.concatenate([jnp.tile(da_qnorm_g * DA_QK_DIM ** -0.5, 2 * DA_HEADS),
                               jnp.tile(da_knorm_g, 2 * DA_HEADS)])
    da_qk = _proj(h2d, w_in[:, c2:c3], BF16, group=DA_QK_DIM, gain=da_gain).reshape(b, s, -1)
    da_v = _proj(h2d, w_in[:, c3:c4], BF16).reshape(b, s, -1)
    gd_qkv = _proj(h2d, w_in[:, c4:c5], F32).reshape(b, s, -1)
    gd_z = _proj(h2d, w_in[:, c5:c6], F32).reshape(b, s, -1)
    w_lg = jnp.pad(w_in[:, c6:c7], ((0, 0), (0, LANES - (c7 - c6))))
    gd_lg = _proj(h2d, w_lg, F32).reshape(b, s, -1)

    ya = _natten(na_qk, na_v, na_rpb)
    lam_init = 0.8 - 0.6 * math.exp(-0.3 * l)
    yb = _diff_attn(da_qk, da_v, da_lambda, da_subln_g, lam_init)
    yc = _gated_deltanet(gd_qkv, gd_z, gd_lg, gdn_conv_w, gdn_a_log, gdn_dt_bias, gdn_norm_g)

    x2d = _merge(h2d, ya.reshape(t, -1), yb.reshape(t, -1), yc.reshape(t, -1),
                 gate_w, gate_b, w_branch_a, w_branch_b, w_branch_c, w_out,
                 x.reshape(t, d), gt1.reshape(b, 1, d), s)

    h2 = _norm_mod(x2d.reshape(b, s, d), norm2_g, sc2, sh2)
    x2d = _ffn(h2.reshape(t, d), ffn_w1, ffn_w3, ffn_w2, x2d, gt2.reshape(b, 1, d), s)
    return x2d.reshape(b, s, d)


def kernel(x, c, ada_w, ada_b, norm1_g, norm2_g, w_in, gate_w, gate_b, na_qnorm_g, na_knorm_g, na_rpb, da_qnorm_g, da_knorm_g, da_lambda, da_subln_g, gdn_conv_w, gdn_a_log, gdn_dt_bias, gdn_norm_g, w_branch_a, w_branch_b, w_branch_c, w_out, ffn_w1, ffn_w3, ffn_w2):
    depth = ada_w.shape[0]
    mod = _adaln(c, ada_w, ada_b)
    per_layer = (norm1_g, norm2_g, w_in, gate_w, gate_b, na_qnorm_g, na_knorm_g, na_rpb,
                 da_qnorm_g, da_knorm_g, da_lambda, da_subln_g, gdn_conv_w, gdn_a_log,
                 gdn_dt_bias, gdn_norm_g, w_branch_a, w_branch_b, w_branch_c, w_out,
                 ffn_w1, ffn_w3, ffn_w2)
    for l in range(depth):
        x = _layer(l, x, mod[l], *[p[l] for p in per_layer])
    return x
```

```python
import functools
import math
from typing import NamedTuple

import jax
import jax.numpy as jnp
from jax import lax
from jax.experimental import pallas as pl
from jax.experimental.pallas import tpu as pltpu

F32 = jnp.float32
BF16 = jnp.bfloat16

GRID_W = 64
HEAD_DIM = 128
NA_HEADS = 4
NA_WIN_R = 8
NA_WIN_C = 16
DA_HEADS = 4
DA_QK_DIM = 64
DA_V_DIM = 128
GDN_HEADS = 8
GDN_DK = 128
GDN_DV = 128
GDN_CONV = 5
GDN_CHUNK = 64
GDN_SUPER = 256
N_BRANCHES = 3
RMS_EPS = 1e-6

NA_W = NA_HEADS * HEAD_DIM
DA_QK_W = DA_HEADS * 2 * DA_QK_DIM
DA_V_W = DA_HEADS * DA_V_DIM
GDN_K_W = GDN_HEADS * GDN_DK
GDN_V_W = GDN_HEADS * GDN_DV

LANES = 128
VMEM_LIMIT = 56 * 1024 * 1024
NEG_BIG = -1e30

_LOG2_LANES = LANES.bit_length() - 1
_LOG2_CHUNK = GDN_CHUNK.bit_length() - 1

_NT = (((1,), (1,)), ((), ()))
_TN = (((0,), (0,)), ((), ()))


def _cparams(sem):
    return pltpu.CompilerParams(dimension_semantics=sem, vmem_limit_bytes=VMEM_LIMIT)


def _sigmoid(x):
    return 1.0 / (1.0 + jnp.exp(-x))


def _dot(a, b):
    return jnp.dot(a.astype(BF16), b.astype(BF16), preferred_element_type=F32)


def _dot_nt(a, b):
    return lax.dot_general(a.astype(BF16), b.astype(BF16), _NT, preferred_element_type=F32)


def _mod_kernel(c_ref, w_ref, b_ref, o_ref):
    c = c_ref[...]
    a = c * _sigmoid(c)
    o_ref[0] = jnp.dot(a, w_ref[0], preferred_element_type=F32,
                       precision=lax.Precision.HIGHEST) + b_ref[0]


def _adaln(c, ada_w, ada_b):
    depth, d, n = ada_w.shape
    b = c.shape[0]
    rows = 8
    cp = jnp.zeros((rows, d), F32).at[:b].set(c)
    tn = 1024
    out = pl.pallas_call(
        _mod_kernel,
        grid=(depth, n // tn),
        in_specs=[pl.BlockSpec((rows, d), lambda l, j: (0, 0)),
                  pl.BlockSpec((1, d, tn), lambda l, j: (l, 0, j)),
                  pl.BlockSpec((1, 1, tn), lambda l, j: (l, 0, j))],
        out_specs=pl.BlockSpec((1, rows, tn), lambda l, j: (l, 0, j)),
        out_shape=jax.ShapeDtypeStruct((depth, rows, n), F32),
        compiler_params=_cparams(("arbitrary", "arbitrary")),
        name="adaln_mod",
    )(cp, ada_w, ada_b.reshape(depth, 1, n))
    return out[:, :b]


def _norm_kernel(x_ref, g_ref, sc_ref, sh_ref, o_ref):
    x = x_ref[0]
    y = x * lax.rsqrt(jnp.mean(x * x, axis=-1, keepdims=True) + RMS_EPS) * g_ref[...]
    o_ref[0] = (y * (1.0 + sc_ref[0]) + sh_ref[0]).astype(o_ref.dtype)


def _norm_mod(x, g, sc, sh):
    b, s, d = x.shape
    ts = min(512, s)
    return pl.pallas_call(
        _norm_kernel,
        grid=(b, s // ts),
        in_specs=[pl.BlockSpec((1, ts, d), lambda i, j: (i, j, 0)),
                  pl.BlockSpec((1, d), lambda i, j: (0, 0)),
                  pl.BlockSpec((1, 1, d), lambda i, j: (i, 0, 0)),
                  pl.BlockSpec((1, 1, d), lambda i, j: (i, 0, 0))],
        out_specs=pl.BlockSpec((1, ts, d), lambda i, j: (i, j, 0)),
        out_shape=jax.ShapeDtypeStruct((b, s, d), BF16),
        compiler_params=_cparams(("parallel", "parallel")),
        name="norm_mod",
    )(x, g.reshape(1, d), sc.reshape(b, 1, d), sh.reshape(b, 1, d))


def _proj_kernel(h_ref, w_ref, *rest, group):
    acc = jnp.dot(h_ref[...], w_ref[...], preferred_element_type=F32)
    if group is None:
        (o_ref,) = rest
        o_ref[...] = acc.astype(o_ref.dtype)
        return
    gain_ref, o_ref = rest
    tn = acc.shape[1]
    for s in range(tn // LANES):
        sl = slice(s * LANES, (s + 1) * LANES)
        a = acc[:, sl]
        sq = a * a
        if group == LANES:
            ms = jnp.sum(sq, axis=-1, keepdims=True) * (1.0 / LANES)
        else:
            lo = lax.broadcasted_iota(jnp.int32, a.shape, 1) < group
            s_lo = jnp.sum(jnp.where(lo, sq, 0.0), axis=-1, keepdims=True)
            s_hi = jnp.sum(jnp.where(lo, 0.0, sq), axis=-1, keepdims=True)
            ms = jnp.where(lo, s_lo, s_hi) * (1.0 / group)
        y = a * lax.rsqrt(ms + RMS_EPS) * gain_ref[:, sl]
        o_ref[:, sl] = y.astype(o_ref.dtype)


def _proj(h2d, w, out_dtype, group=None, gain=None):
    t, d = h2d.shape
    n = w.shape[1]
    tm = min(1024, t)
    tn = min(512, n)
    in_specs = [pl.BlockSpec((tm, d), lambda i, j: (i, 0)),
                pl.BlockSpec((d, tn), lambda i, j: (0, j))]
    args = [h2d, w.astype(BF16)]
    if group is not None:
        assert group in (LANES // 2, LANES)
        in_specs.append(pl.BlockSpec((1, tn), lambda i, j: (0, j)))
        args.append(gain.reshape(1, n).astype(F32))
    return pl.pallas_call(
        functools.partial(_proj_kernel, group=group),
        grid=(t // tm, n // tn),
        in_specs=in_specs,
        out_specs=pl.BlockSpec((tm, tn), lambda i, j: (i, j)),
        out_shape=jax.ShapeDtypeStruct((t, n), out_dtype),
        compiler_params=_cparams(("parallel", "arbitrary")),
        name="proj",
    )(*args)


NA_ROWS_PER_STEP = 8
NA_BAND = NA_WIN_R * GRID_W


def _na_bias_table(rpb):
    h = rpb.shape[0]
    w = jnp.arange(GRID_W)
    col_start = jnp.clip(w - NA_WIN_C // 2, 0, GRID_W - NA_WIN_C)
    ck = jnp.arange(GRID_W)
    inwin = (ck[None, :] >= col_start[:, None]) & (ck[None, :] < col_start[:, None] + NA_WIN_C)
    cbi = jnp.clip(ck[None, :] - w[:, None] + (NA_WIN_C - 1), 0, 2 * NA_WIN_C - 2)
    e = jnp.arange(NA_WIN_R)
    j = jnp.arange(NA_WIN_R)
    rbi = j[None, :] + (NA_WIN_R - 1) - e[:, None]
    tab = rpb[:, rbi]
    tab = tab[:, :, :, cbi]
    tab = jnp.where(inwin[None, None, None], tab, NEG_BIG)
    return tab.transpose(1, 0, 3, 2, 4).reshape(NA_WIN_R, h, GRID_W, NA_BAND).astype(F32)


def _na_kernel(q_ref, kp_ref, kc_ref, kn_ref, vp_ref, vc_ref, vn_ref, tab_ref, o_ref,
               kbuf, vbuf, *, rows):
    rb = pl.program_id(1)
    blk = NA_ROWS_PER_STEP * GRID_W
    for n, (kr, vr) in enumerate(((kp_ref, vp_ref), (kc_ref, vc_ref), (kn_ref, vn_ref))):
        kbuf[n * blk:(n + 1) * blk, :] = kr[0]
        vbuf[n * blk:(n + 1) * blk, :] = vr[0]
    for i in range(NA_ROWS_PER_STEP):
        r = rb * NA_ROWS_PER_STEP + i
        r_start = jnp.clip(r - NA_WIN_R // 2, 0, rows - NA_WIN_R)
        e = r - r_start
        loc = pl.multiple_of((r_start - (rb - 1) * NA_ROWS_PER_STEP) * GRID_W, GRID_W)
        rs = slice(i * GRID_W, (i + 1) * GRID_W)
        for h in range(NA_HEADS):
            hs = slice(h * HEAD_DIM, (h + 1) * HEAD_DIM)
            q = q_ref[0, rs, hs]
            kb = kbuf[pl.ds(loc, NA_BAND), hs]
            vb = vbuf[pl.ds(loc, NA_BAND), hs]
            s = lax.dot_general(q, kb, _NT, preferred_element_type=F32) + tab_ref[e, h]
            m = jnp.max(s, axis=-1, keepdims=True)
            p = jnp.exp(s - m)
            l = jnp.sum(p, axis=-1, keepdims=True)
            o = jnp.dot(p.astype(BF16), vb, preferred_element_type=F32) / l
            o_ref[0, rs, hs] = o.astype(o_ref.dtype)


def _natten(qk, v, rpb):
    b, s, _ = v.shape
    rows = s // GRID_W
    blk = NA_ROWS_PER_STEP * GRID_W
    nrb = s // blk
    tab = _na_bias_table(rpb)

    def at(off, col):
        return lambda i, j: (i, jnp.clip(j + off, 0, nrb - 1), col)

    return pl.pallas_call(
        functools.partial(_na_kernel, rows=rows),
        grid=(b, nrb),
        in_specs=[pl.BlockSpec((1, blk, NA_W), at(0, 0)),
                  pl.BlockSpec((1, blk, NA_W), at(-1, 1)),
                  pl.BlockSpec((1, blk, NA_W), at(0, 1)),
                  pl.BlockSpec((1, blk, NA_W), at(1, 1)),
                  pl.BlockSpec((1, blk, NA_W), at(-1, 0)),
                  pl.BlockSpec((1, blk, NA_W), at(0, 0)),
                  pl.BlockSpec((1, blk, NA_W), at(1, 0)),
                  pl.BlockSpec(tab.shape, lambda i, j: (0, 0, 0, 0))],
        out_specs=pl.BlockSpec((1, blk, NA_W), at(0, 0)),
        out_shape=jax.ShapeDtypeStruct((b, s, NA_W), BF16),
        scratch_shapes=[pltpu.VMEM((3 * blk, NA_W), BF16), pltpu.VMEM((3 * blk, NA_W), BF16)],
        compiler_params=_cparams(("parallel", "parallel")),
        name="natten",
    )(qk, qk, qk, qk, v, v, v, tab)


DA_TILE = 512
DA_POS_SPLIT = 16
DA_ONES_ROWS = 16


def _da_slopes():
    return [2.0 ** (-8.0 * (i + 1) / DA_HEADS) for i in range(DA_HEADS)]


def _da_pos_tables(tile):
    slopes = jnp.asarray(_da_slopes(), F32)[:, None]
    pos = jnp.arange(tile)
    hi = (pos - pos % DA_POS_SPLIT).astype(F32)[None, :]
    lo = (pos % DA_POS_SPLIT).astype(F32)[None, :]
    one = jnp.ones((DA_HEADS, tile), F32)
    zpad = jnp.zeros((DA_HEADS, DA_QK_DIM - 4, tile), F32)
    kfeat = jnp.concatenate([jnp.stack([slopes * hi, slopes * lo, one, one], axis=1), zpad], axis=1)
    qfeat = jnp.concatenate([jnp.stack([one, one, -slopes * hi, -slopes * lo], axis=1), zpad], axis=1)
    dist = jnp.abs(pos[:, None] - pos[None, :]).astype(F32)
    bdiag = -slopes[:, :, None] * dist[None]
    return qfeat.astype(BF16), kfeat.astype(BF16), bdiag


def _da_kernel(qt_ref, ka_ref, vt_ref, qpos_ref, ones_ref, bdiag_ref, slope_ref, lam_ref, g_ref,
               o_ref, qa_sc, st0_sc, st1_sc, mt_sc, m_sc, acc_sc, *, tile, lam_init):
    hd = pl.program_id(1)
    qi = pl.program_id(2)
    nk = vt_ref.shape[2]
    nv = nk - 1
    half = DA_QK_DIM

    qt = qt_ref[0]
    qpos = qpos_ref[0]
    for sel, pp in enumerate((qpos, -qpos, jnp.zeros_like(qpos))):
        qa_sc[sel, 0] = jnp.concatenate([qt[0:half], pp], axis=0)
        qa_sc[sel, 1] = jnp.concatenate([pp, qt[half:2 * half]], axis=0)
    slope = slope_ref[hd]
    ones_rows = ones_ref[...]

    def softmax_pv(c, st, mt, shift, vat):
        m_old = m_sc[c]
        m_new = jnp.maximum(m_old, mt + shift)
        alpha = jnp.exp(m_old - m_new)
        pt = jnp.exp(st - (m_new - shift)).astype(BF16)
        acc_sc[c] = alpha * acc_sc[c] + jnp.dot(vat, pt, preferred_element_type=F32)
        m_sc[c] = m_new

    m_sc[...] = jnp.full(m_sc.shape, NEG_BIG, F32)
    acc_sc[...] = jnp.zeros(acc_sc.shape, F32)
    vat = jnp.concatenate([vt_ref[0, 0, qi], ones_rows], axis=0)
    start = pl.multiple_of(qi * tile, tile)
    for c in range(2):
        ka = ka_ref[0, 0, c, pl.ds(start, tile), :]
        st = jnp.dot(ka, qa_sc[2, c], preferred_element_type=F32) + bdiag_ref[0]
        softmax_pv(c, st, jnp.max(st, axis=0, keepdims=True), 0.0, vat)

    if nv == 0:
        pass
    else:
        def key_tile(v):
            vc = jnp.minimum(v, nv - 1)
            after = (vc >= qi).astype(jnp.int32)
            return vc + after, after

        def score_stage(v, st_sc, slot):
            t, after = key_tile(v)
            start = pl.multiple_of(t * tile, tile)
            for c in range(2):
                ka = ka_ref[0, 0, c, pl.ds(start, tile), :]
                st = jnp.dot(ka, qa_sc[after, c], preferred_element_type=F32)
                st_sc[c] = st
                mt_sc[slot, c] = jnp.max(st, axis=0, keepdims=True)

        def softmax_stage(v, st_sc, slot):
            t, _ = key_tile(v)
            shift = jnp.where(v < nv, -slope * (tile * jnp.abs(t - qi)).astype(F32), NEG_BIG)
            vat = jnp.concatenate([vt_ref[0, 0, t], ones_rows], axis=0)
            for c in range(2):
                softmax_pv(c, st_sc[c], mt_sc[slot, c], shift, vat)

        score_stage(0, st0_sc, 0)

        def pair(j, carry):
            score_stage(2 * j + 1, st1_sc, 1)
            softmax_stage(2 * j, st0_sc, 0)
            score_stage(2 * j + 2, st0_sc, 0)
            softmax_stage(2 * j + 1, st1_sc, 1)
            return carry

        lax.fori_loop(0, (nv + 1) // 2, pair, 0)

    lp = lam_ref[...]
    lam = (jnp.exp(jnp.sum(lp[0:1] * lp[1:2], axis=-1, keepdims=True))
           - jnp.exp(jnp.sum(lp[2:3] * lp[3:4], axis=-1, keepdims=True)) + lam_init)
    a0 = acc_sc[0]
    a1 = acc_sc[1]
    o0 = a0[0:DA_V_DIM] / a0[DA_V_DIM:DA_V_DIM + 1]
    o1 = a1[0:DA_V_DIM] / a1[DA_V_DIM:DA_V_DIM + 1]
    a = o0 - lam * o1
    y = a * lax.rsqrt(jnp.mean(a * a, axis=0, keepdims=True) + RMS_EPS) * g_ref[...]
    o_ref[0] = (y * (1.0 - lam_init)).T.astype(o_ref.dtype)


def _diff_attn(qk, v, lam_p, subln_g, lam_init):
    b, s, _ = v.shape
    tile = min(DA_TILE, s)
    nt = s // tile
    nh = DA_HEADS
    qfeat, kfeat, bdiag = _da_pos_tables(tile)
    slopes = jnp.asarray(_da_slopes(), F32)

    qt = qk[:, :, :DA_QK_W].transpose(0, 2, 1)
    k4 = qk[:, :, DA_QK_W:].reshape(b, s, nh, 2, DA_QK_DIM)
    kp = jnp.broadcast_to(jnp.tile(kfeat.transpose(0, 2, 1), (1, nt, 1))[None], (b, nh, s, DA_QK_DIM))
    k0 = jnp.concatenate([k4[:, :, :, 0].transpose(0, 2, 1, 3), kp], axis=-1)
    k1 = jnp.concatenate([kp, k4[:, :, :, 1].transpose(0, 2, 1, 3)], axis=-1)
    ka = jnp.stack([k0, k1], axis=2)
    vt = v.reshape(b, nt, tile, nh, DA_V_DIM).transpose(0, 3, 1, 4, 2)
    ones_rows = jnp.zeros((DA_ONES_ROWS, tile), BF16).at[0].set(1.0)

    return pl.pallas_call(
        functools.partial(_da_kernel, tile=tile, lam_init=lam_init),
        grid=(b, nh, nt),
        in_specs=[pl.BlockSpec((1, LANES, tile), lambda i, h, q: (i, h, q)),
                  pl.BlockSpec((1, 1, 2, s, LANES), lambda i, h, q: (i, h, 0, 0, 0)),
                  pl.BlockSpec((1, 1, nt, DA_V_DIM, tile), lambda i, h, q: (i, h, 0, 0, 0)),
                  pl.BlockSpec((1, DA_QK_DIM, tile), lambda i, h, q: (h, 0, 0)),
                  pl.BlockSpec((DA_ONES_ROWS, tile), lambda i, h, q: (0, 0)),
                  pl.BlockSpec((1, tile, tile), lambda i, h, q: (h, 0, 0)),
                  pl.BlockSpec(memory_space=pltpu.SMEM),
                  pl.BlockSpec((4, DA_QK_DIM), lambda i, h, q: (0, 0)),
                  pl.BlockSpec((DA_V_DIM, 1), lambda i, h, q: (0, 0))],
        out_specs=pl.BlockSpec((1, tile, DA_V_DIM), lambda i, h, q: (i, q, h)),
        out_shape=jax.ShapeDtypeStruct((b, s, DA_V_W), BF16),
        scratch_shapes=[pltpu.VMEM((3, 2, LANES, tile), BF16),
                        pltpu.VMEM((2, tile, tile), F32),
                        pltpu.VMEM((2, tile, tile), F32),
                        pltpu.VMEM((2, 2, 1, tile), F32),
                        pltpu.VMEM((2, 1, tile), F32),
                        pltpu.VMEM((2, DA_V_DIM + DA_ONES_ROWS, tile), F32)],
        compiler_params=_cparams(("parallel", "parallel", "arbitrary")),
        name="diff_attn",
    )(qt, ka, vt, qfeat, ones_rows, bdiag, slopes, lam_p.astype(F32),
      subln_g.reshape(DA_V_DIM, 1).astype(F32))


def _gdn_gate_kernel(lg_ref, alog_ref, dt_ref, beta_ref, gc_ref, gt_ref):
    lg = lg_ref[0]
    nh = 2 * GDN_HEADS
    beta_ref[0] = _sigmoid(lg[0:nh])
    z = lg[nh:2 * nh] + dt_ref[...]
    sp = jnp.maximum(z, 0.0) + jnp.log(1.0 + jnp.exp(-jnp.abs(z)))
    g = -jnp.exp(alog_ref[...]) * sp
    ts = g.shape[1]
    pos = lax.broadcasted_iota(jnp.int32, g.shape, 1) & (GDN_CHUNK - 1)
    pre = g
    suf = g
    sh = 1
    while sh < GDN_CHUNK:
        pre = pre + jnp.where(pos >= sh, pltpu.roll(pre, sh, axis=1), 0.0)
        suf = suf + jnp.where(pos < GDN_CHUNK - sh, pltpu.roll(suf, ts - sh, axis=1), 0.0)
        sh *= 2
    bwd = lax.broadcasted_iota(jnp.int32, g.shape, 0) >= GDN_HEADS
    gc_ref[0] = jnp.where(bwd, suf, pre)
    gt_ref[0] = jnp.where(bwd, pre, suf) - g


def _gdn_gates(lgt, a_log, dt_bias):
    b, _, s = lgt.shape
    nh = 2 * GDN_HEADS
    ts = min(2048, s)
    shp = jax.ShapeDtypeStruct((b, nh, s), F32)
    spec = pl.BlockSpec((1, nh, ts), lambda i, j: (i, 0, j))
    return pl.pallas_call(
        _gdn_gate_kernel,
        grid=(b, s // ts),
        in_specs=[pl.BlockSpec((1, 2 * nh, ts), lambda i, j: (i, 0, j)),
                  pl.BlockSpec((nh, 1), lambda i, j: (0, 0)),
                  pl.BlockSpec((nh, 1), lambda i, j: (0, 0))],
        out_specs=[spec, spec, spec],
        out_shape=[shp, shp, shp],
        compiler_params=_cparams(("parallel", "parallel")),
        name="gdn_gates",
    )(lgt, a_log.reshape(nh, 1).astype(F32), dt_bias.reshape(nh, 1).astype(F32))


GDN_HALO = 8


def _gdn_conv_kernel(x_ref, p_ref, n_ref, w_ref, o_ref, ext, *, ts):
    cg = pl.program_id(0)
    i = pl.program_id(2)
    last = pl.num_programs(2) - 1
    ext[0:GDN_HALO, :] = jnp.where(i > 0, p_ref[0], 0.0)
    ext[GDN_HALO:GDN_HALO + ts, :] = x_ref[0]
    ext[GDN_HALO + ts:2 * GDN_HALO + ts, :] = jnp.where(i < last, n_ref[0], 0.0)
    y = None
    for tap in range(GDN_CONV):
        off = GDN_HALO + tap - GDN_CONV // 2
        term = ext[off:off + ts, :] * w_ref[0, tap:tap + 1, :]
        y = term if y is None else y + term
    y = y * _sigmoid(y)
    qscale = jnp.where(cg == 0, GDN_DK ** -0.5, 1.0).astype(F32)
    is_v = cg == 2
    for h in range(GDN_HEADS):
        hs = slice(h * GDN_DK, (h + 1) * GDN_DK)
        a = y[:, hs]
        nrm = lax.rsqrt(jnp.sum(a * a, axis=-1, keepdims=True) + RMS_EPS) * qscale
        o_ref[0, 0, :, hs] = a * jnp.where(is_v, 1.0, nrm)


def _gdn_conv(u, conv_w):
    b, s, _ = u.shape
    cw = GDN_K_W
    ts = min(512, s)
    nb = s // GDN_HALO
    per = ts // GDN_HALO
    w3 = conv_w.reshape(GDN_CONV, 3, cw).transpose(1, 0, 2)
    w3 = jnp.pad(w3, ((0, 0), (0, 8 - GDN_CONV), (0, 0))).astype(F32)
    return pl.pallas_call(
        functools.partial(_gdn_conv_kernel, ts=ts),
        grid=(3, b, s // ts),
        in_specs=[pl.BlockSpec((1, ts, cw), lambda c, i, j: (i, j, c)),
                  pl.BlockSpec((1, GDN_HALO, cw),
                               lambda c, i, j: (i, jnp.maximum(j * per - 1, 0), c)),
                  pl.BlockSpec((1, GDN_HALO, cw),
                               lambda c, i, j: (i, jnp.minimum((j + 1) * per, nb - 1), c)),
                  pl.BlockSpec((1, 8, cw), lambda c, i, j: (c, 0, 0))],
        out_specs=pl.BlockSpec((1, 1, ts, cw), lambda c, i, j: (c, i, j, 0)),
        out_shape=jax.ShapeDtypeStruct((3, b, s, cw), F32),
        scratch_shapes=[pltpu.VMEM((ts + 2 * GDN_HALO, cw), F32)],
        compiler_params=_cparams(("parallel", "parallel", "parallel")),
        name="gdn_conv",
    )(u, u, u, w3)


def _split_dot(a, b):
    a_hi = a.astype(BF16)
    b_hi = b.astype(BF16)
    a_lo = (a - a_hi.astype(F32)).astype(BF16)
    b_lo = (b - b_hi.astype(F32)).astype(BF16)
    return (jnp.dot(a_hi, b_hi, preferred_element_type=F32)
            + jnp.dot(a_hi, b_lo, preferred_element_type=F32)
            + jnp.dot(a_lo, b_hi, preferred_element_type=F32))


class _GdnChain(NamedTuple):
    q: jax.Array
    k: jax.Array
    v: jax.Array
    cols: jax.Array
    gr: jax.Array
    rev: bool
    state: object
    out: object


def _gdn_chains(chains):
    sc = GDN_SUPER
    nchunk = sc // GDN_CHUNK
    ii = lax.broadcasted_iota(jnp.int32, (sc, sc), 0)
    jj = lax.broadcasted_iota(jnp.int32, (sc, sc), 1)
    same = (ii >> _LOG2_CHUNK) == (jj >> _LOG2_CHUNK)
    eye = jnp.where(ii == jj, 1.0, 0.0).astype(F32)
    masks = {False: (same & (ii >= jj), same & (ii > jj)),
             True: (same & (ii <= jj), same & (ii < jj))}

    def lanes(col):
        return jnp.broadcast_to(col, (sc, LANES))

    bcast = [(lanes(ch.cols[:, 0:1]), lanes(ch.cols[:, 1:2]), lanes(ch.cols[:, 2:3])) for ch in chains]

    decays, kbs, a_mats = [], [], []
    for ch, (beta, gc, _) in zip(chains, bcast):
        incl, strict = masks[ch.rev]
        gc2 = jnp.concatenate([gc, gc], axis=1)
        decay = jnp.exp(jnp.where(incl, gc2 - ch.gr, NEG_BIG))
        kb = ch.k * beta
        decays.append(decay)
        kbs.append(kb)
        a_mats.append(jnp.where(strict, _dot_nt(kb, ch.k) * decay, 0.0))

    xs = [-a for a in a_mats]
    ts = [eye + x for x in xs]
    span = 2
    while span < GDN_CHUNK:
        xs = [_dot(x, x) for x in xs]
        ts = [t + _dot(t, x) for t, x in zip(ts, xs)]
        span *= 2
    resids = [eye - t - _split_dot(a, t) for a, t in zip(a_mats, ts)]
    ts = [t + _dot(t, r) for t, r in zip(ts, resids)]

    pre = []
    for ch, (beta, gc, gt), decay, kb, t in zip(chains, bcast, decays, kbs, ts):
        egc = jnp.exp(gc)
        pre.append(dict(u=_dot(t, ch.v * beta), w=_dot(t, kb * egc), qg=ch.q * egc,
                        ai=_dot_nt(ch.q, ch.k) * decay, kt=ch.k * jnp.exp(gt),
                        gl=jnp.exp(gc + gt)))

    states = [ch.state[...] for ch in chains]
    outs = [[None] * nchunk for _ in chains]
    for step in range(nchunk):
        for n, (ch, p) in enumerate(zip(chains, pre)):
            j = nchunk - 1 - step if ch.rev else step
            sl = slice(j * GDN_CHUNK, (j + 1) * GDN_CHUNK)
            s_mat = states[n]
            v_new = p["u"][sl] - _dot(p["w"][sl], s_mat)
            outs[n][j] = _dot(p["qg"][sl], s_mat) + _dot(p["ai"][sl, sl], v_new)
            states[n] = s_mat * p["gl"][j * GDN_CHUNK:j * GDN_CHUNK + 1, :] + lax.dot_general(
                p["kt"][sl].astype(BF16), v_new.astype(BF16), _TN, preferred_element_type=F32)
    for ch, s_mat, o in zip(chains, states, outs):
        ch.state[...] = s_mat
        ch.out[...] = jnp.concatenate(o, axis=0)


GDN_HEADS_PER_STEP = 2


def _gdn_scan_kernel(qf_ref, kf_ref, vf_ref, cf_ref, rf_ref, qb_ref, kb_ref, vb_ref, cb_ref, rb_ref,
                     of_ref, ob_ref, state):
    hg = pl.program_id(1)

    @pl.when(pl.program_id(2) == 0)
    def _():
        state[...] = jnp.zeros(state.shape, F32)

    per_group = 3 * GDN_HEADS_PER_STEP
    shift = (LANES - per_group * hg) & (LANES - 1)
    cols_fb = (pltpu.roll(cf_ref[0, 0], shift, axis=1), pltpu.roll(cb_ref[0, 0], shift, axis=1))

    chains = []
    for i in range(GDN_HEADS_PER_STEP):
        hd = hg * GDN_HEADS_PER_STEP + i
        ks = slice(i * GDN_DK, (i + 1) * GDN_DK)
        vs = slice(i * GDN_DV, (i + 1) * GDN_DV)
        for rev, (q_ref, k_ref, v_ref, r_ref, o_ref) in enumerate(
                ((qf_ref, kf_ref, vf_ref, rf_ref, of_ref), (qb_ref, kb_ref, vb_ref, rb_ref, ob_ref))):
            chains.append(_GdnChain(
                q=q_ref[0, 0, :, ks], k=k_ref[0, 0, :, ks], v=v_ref[0, 0, :, vs],
                cols=cols_fb[rev][:, 3 * i:3 * i + 3],
                gr=r_ref[0, 0, pl.ds(hd, 1), :], rev=bool(rev),
                state=state.at[2 * i + rev], out=o_ref.at[0, :, vs]))
    _gdn_chains(chains)


def _gdn_scan(qkv, cols, rows):
    _, b, s, _ = qkv.shape
    sc = GDN_SUPER
    nsc = s // sc
    hps = GDN_HEADS_PER_STEP

    def specs(d):
        seq = (lambda n: nsc - 1 - n) if d else (lambda n: n)
        return [pl.BlockSpec((1, 1, sc, hps * GDN_DK), lambda i, h, n: (0, i, seq(n), h)),
                pl.BlockSpec((1, 1, sc, hps * GDN_DK), lambda i, h, n: (1, i, seq(n), h)),
                pl.BlockSpec((1, 1, sc, hps * GDN_DV), lambda i, h, n: (2, i, seq(n), h)),
                pl.BlockSpec((1, 1, sc, LANES), lambda i, h, n: (i, d, seq(n), 0)),
                pl.BlockSpec((1, 1, GDN_HEADS, sc), lambda i, h, n: (i, d, 0, seq(n)))]

    out_shape = jax.ShapeDtypeStruct((b, s, GDN_V_W), F32)
    return pl.pallas_call(
        _gdn_scan_kernel,
        grid=(b, GDN_HEADS // hps, nsc),
        in_specs=specs(0) + specs(1),
        out_specs=[pl.BlockSpec((1, sc, hps * GDN_DV), lambda i, h, n: (i, n, h)),
                   pl.BlockSpec((1, sc, hps * GDN_DV), lambda i, h, n: (i, nsc - 1 - n, h))],
        out_shape=[out_shape, out_shape],
        scratch_shapes=[pltpu.VMEM((2 * hps, GDN_DK, GDN_DV), F32)],
        compiler_params=_cparams(("parallel", "parallel", "arbitrary")),
        name="gdn_scan",
    )(qkv, qkv, qkv, cols, rows, qkv, qkv, qkv, cols, rows)


def _gdn_out_kernel(of_ref, ob_ref, z_ref, g_ref, o_ref):
    o = of_ref[0] + ob_ref[0]
    z = z_ref[0].astype(F32)
    gate = z * _sigmoid(z)
    for h in range(GDN_HEADS):
        hs = slice(h * GDN_DV, (h + 1) * GDN_DV)
        a = o[:, hs]
        y = a * lax.rsqrt(jnp.mean(a * a, axis=-1, keepdims=True) + RMS_EPS) * g_ref[...]
        o_ref[0, :, hs] = (y * gate[:, hs]).astype(o_ref.dtype)


def _gdn_out(o_f, o_b, z, norm_g):
    b, s, w = o_f.shape
    ts = min(512, s)
    spec = pl.BlockSpec((1, ts, w), lambda i, j: (i, j, 0))
    return pl.pallas_call(
        _gdn_out_kernel,
        grid=(b, s // ts),
        in_specs=[spec, spec, spec, pl.BlockSpec((1, GDN_DV), lambda i, j: (0, 0))],
        out_specs=spec,
        out_shape=jax.ShapeDtypeStruct((b, s, w), BF16),
        compiler_params=_cparams(("parallel", "parallel")),
        name="gdn_out",
    )(o_f, o_b, z, norm_g.reshape(1, GDN_DV).astype(F32))


def _gated_deltanet(u_qkv, z, lg, conv_w, a_log, dt_bias, norm_g):
    b, s, _ = u_qkv.shape
    nh = GDN_HEADS
    lgt = lg[:, :, :4 * nh].transpose(0, 2, 1)
    beta, gc, gt = _gdn_gates(lgt, a_log, dt_bias)
    rows = gc.reshape(b, 2, nh, s)
    cols = jnp.stack([beta, gc, gt], axis=-1).reshape(b, 2, nh, s, 3)
    cols = cols.transpose(0, 1, 3, 2, 4).reshape(b, 2, s, 3 * nh)
    cols = jnp.pad(cols, ((0, 0), (0, 0), (0, 0), (0, LANES - 3 * nh)))
    qkv = _gdn_conv(u_qkv, conv_w)
    o_f, o_b = _gdn_scan(qkv, cols, rows)
    return _gdn_out(o_f, o_b, z, norm_g)


def _merge_kernel(h_ref, ya_ref, yb_ref, yc_ref, gw0_ref, gw1_ref, gw2_ref,
                  gb0_ref, gb1_ref, gb2_ref, wa_ref, wb_ref, wc_ref, wo_ref,
                  x_ref, gt_ref, o_ref, acc):
    j = pl.program_id(1)
    h = h_ref[...]

    def gate(gw_ref, gb_ref):
        return _sigmoid(jnp.dot(h, gw_ref[...], preferred_element_type=F32) + gb_ref[...])

    m = (gate(gw0_ref, gb0_ref) * jnp.dot(ya_ref[...], wa_ref[...], preferred_element_type=F32)
         + gate(gw1_ref, gb1_ref) * jnp.dot(yb_ref[...], wb_ref[...], preferred_element_type=F32)
         + gate(gw2_ref, gb2_ref) * jnp.dot(yc_ref[...], wc_ref[...], preferred_element_type=F32))
    part = jnp.dot(m.astype(BF16), wo_ref[...], preferred_element_type=F32)

    @pl.when(j == 0)
    def _():
        acc[...] = part

    @pl.when(j > 0)
    def _():
        acc[...] += part

    @pl.when(j == pl.num_programs(1) - 1)
    def _():
        o_ref[...] = x_ref[...] + gt_ref[0] * acc[...]


def _merge(h2d, ya, yb, yc, gate_w, gate_b, wa, wb, wc, w_out, x2d, gt, seq):
    t, d = x2d.shape
    tm = min(512, seq)
    tn = 512
    nj = d // tn
    per_batch = seq // tm
    gw = gate_w.astype(BF16)
    gb = gate_b.reshape(1, N_BRANCHES * d).astype(F32)
    row = lambda w: pl.BlockSpec((tm, w), lambda i, j: (i, 0))
    gws = [pl.BlockSpec((d, tn), (lambda i, j, r=r: (0, r * nj + j))) for r in range(N_BRANCHES)]
    gbs = [pl.BlockSpec((1, tn), (lambda i, j, r=r: (0, r * nj + j))) for r in range(N_BRANCHES)]
    return pl.pallas_call(
        _merge_kernel,
        grid=(t // tm, nj),
        in_specs=[row(d), row(ya.shape[1]), row(yb.shape[1]), row(yc.shape[1]),
                  *gws, *gbs,
                  pl.BlockSpec((wa.shape[0], tn), lambda i, j: (0, j)),
                  pl.BlockSpec((wb.shape[0], tn), lambda i, j: (0, j)),
                  pl.BlockSpec((wc.shape[0], tn), lambda i, j: (0, j)),
                  pl.BlockSpec((tn, d), lambda i, j: (j, 0)),
                  row(d),
                  pl.BlockSpec((1, 1, d), lambda i, j: (i // per_batch, 0, 0))],
        out_specs=row(d),
        out_shape=jax.ShapeDtypeStruct((t, d), F32),
        scratch_shapes=[pltpu.VMEM((tm, d), F32)],
        compiler_params=_cparams(("parallel", "arbitrary")),
        name="merge",
    )(h2d, ya, yb, yc, gw, gw, gw, gb, gb, gb,
      wa.astype(BF16), wb.astype(BF16), wc.astype(BF16), w_out.astype(BF16), x2d, gt)


def _ffn_kernel(h_ref, w1_ref, w3_ref, w2_ref, x_ref, gt_ref, o_ref, acc):
    j = pl.program_id(1)
    h = h_ref[...]
    a = jnp.dot(h, w1_ref[...], preferred_element_type=F32)
    b = jnp.dot(h, w3_ref[...], preferred_element_type=F32)
    part = jnp.dot((a * _sigmoid(a) * b).astype(BF16), w2_ref[...], preferred_element_type=F32)

    @pl.when(j == 0)
    def _():
        acc[...] = part

    @pl.when(j > 0)
    def _():
        acc[...] += part

    @pl.when(j == pl.num_programs(1) - 1)
    def _():
        o_ref[...] = x_ref[...] + gt_ref[0] * acc[...]


def _ffn(h2d, w1, w3, w2, x2d, gt, seq):
    t, d = x2d.shape
    f = w1.shape[1]
    tm = min(512, seq)
    tf = 512
    per_batch = seq // tm
    row = pl.BlockSpec((tm, d), lambda i, j: (i, 0))
    return pl.pallas_call(
        _ffn_kernel,
        grid=(t // tm, f // tf),
        in_specs=[row,
                  pl.BlockSpec((d, tf), lambda i, j: (0, j)),
                  pl.BlockSpec((d, tf), lambda i, j: (0, j)),
                  pl.BlockSpec((tf, d), lambda i, j: (j, 0)),
                  row,
                  pl.BlockSpec((1, 1, d), lambda i, j: (i // per_batch, 0, 0))],
        out_specs=row,
        out_shape=jax.ShapeDtypeStruct((t, d), F32),
        scratch_shapes=[pltpu.VMEM((tm, d), F32)],
        compiler_params=_cparams(("parallel", "arbitrary")),
        name="ffn",
    )(h2d, w1.astype(BF16), w3.astype(BF16), w2.astype(BF16), x2d, gt)


def _layer(l, x, mod, norm1_g, norm2_g, w_in, gate_w, gate_b, na_qnorm_g, na_knorm_g, na_rpb,
           da_qnorm_g, da_knorm_g, da_lambda, da_subln_g, gdn_conv_w, gdn_a_log, gdn_dt_bias,
           gdn_norm_g, w_branch_a, w_branch_b, w_branch_c, w_out, ffn_w1, ffn_w3, ffn_w2):
    b, s, d = x.shape
    t = b * s
    sh1, sc1, gt1, sh2, sc2, gt2 = [mod[:, i * d:(i + 1) * d] for i in range(6)]

    h = _norm_mod(x, norm1_g, sc1, sh1)
    h2d = h.reshape(t, d)

    c0 = 0
    c1 = c0 + 2 * NA_W
    c2 = c1 + NA_W
    c3 = c2 + 2 * DA_QK_W
    c4 = c3 + DA_V_W
    c5 = c4 + 2 * GDN_K_W + GDN_V_W
    c6 = c5 + GDN_V_W
    c7 = c6 + 4 * GDN_HEADS

    na_gain = jnp.concatenate([jnp.tile(na_qnorm_g * HEAD_DIM ** -0.5, NA_HEADS),
                               jnp.tile(na_knorm_g, NA_HEADS)])
    na_qk = _proj(h2d, w_in[:, c0:c1], BF16, group=HEAD_DIM, gain=na_gain).reshape(b, s, -1)
    na_v = _proj(h2d, w_in[:, c1:c2], BF16).reshape(b, s, -1)
    da_gain = jnp.concatenate([jnp.tile(da_qnorm_g * DA_QK_DIM ** -0.5, 2 * DA_HEADS),
                               jnp.tile(da_knorm_g, 2 * DA_HEADS)])
    da_qk = _proj(h2d, w_in[:, c2:c3], BF16, group=DA_QK_DIM, gain=da_gain).reshape(b, s, -1)
    da_v = _proj(h2d, w_in[:, c3:c4], BF16).reshape(b, s, -1)
    gd_qkv = _proj(h2d, w_in[:, c4:c5], F32).reshape(b, s, -1)
    gd_z = _proj(h2d, w_in[:, c5:c6], F32).reshape(b, s, -1)
    w_lg = jnp.pad(w_in[:, c6:c7], ((0, 0), (0, LANES - (c7 - c6))))
    gd_lg = _proj(h2d, w_lg, F32).reshape(b, s, -1)

    ya = _natten(na_qk, na_v, na_rpb)
    lam_init = 0.8 - 0.6 * math.exp(-0.3 * l)
    yb = _diff_attn(da_qk, da_v, da_lambda, da_subln_g, lam_init)
    yc = _gated_deltanet(gd_qkv, gd_z, gd_lg, gdn_conv_w, gdn_a_log, gdn_dt_bias, gdn_norm_g)

    x2d = _merge(h2d, ya.reshape(t, -1), yb.reshape(t, -1), yc.reshape(t, -1),
                 gate_w, gate_b, w_branch_a, w_branch_b, w_branch_c, w_out,
                 x.reshape(t, d), gt1.reshape(b, 1, d), s)

    h2 = _norm_mod(x2d.reshape(b, s, d), norm2_g, sc2, sh2)
    x2d = _ffn(h2.reshape(t, d), ffn_w1, ffn_w3, ffn_w2, x2d, gt2.reshape(b, 1, d), s)
    return x2d.reshape(b, s, d)


def kernel(x, c, ada_w, ada_b, norm1_g, norm2_g, w_in, gate_w, gate_b, na_qnorm_g, na_knorm_g, na_rpb, da_qnorm_g, da_knorm_g, da_lambda, da_subln_g, gdn_conv_w, gdn_a_log, gdn_dt_bias, gdn_norm_g, w_branch_a, w_branch_b, w_branch_c, w_out, ffn_w1, ffn_w3, ffn_w2):
    depth = ada_w.shape[0]
    mod = _adaln(c, ada_w, ada_b)
    per_layer = (norm1_g, norm2_g, w_in, gate_w, gate_b, na_qnorm_g, na_knorm_g, na_rpb,
                 da_qnorm_g, da_knorm_g, da_lambda, da_subln_g, gdn_conv_w, gdn_a_log,
                 gdn_dt_bias, gdn_norm_g, w_branch_a, w_branch_b, w_branch_c, w_out,
                 ffn_w1, ffn_w3, ffn_w2)
    for l in range(depth):
        x = _layer(l, x, mod[l], *[p[l] for p in per_layer])
    return x
```

```python
import functools
import math
from typing import NamedTuple

import jax
import jax.numpy as jnp
from jax import lax
from jax.experimental import pallas as pl
from jax.experimental.pallas import tpu as pltpu

F32 = jnp.float32
BF16 = jnp.bfloat16

GRID_W = 64
HEAD_DIM = 128
NA_HEADS = 4
NA_WIN_R = 8
NA_WIN_C = 16
DA_HEADS = 4
DA_QK_DIM = 64
DA_V_DIM = 128
GDN_HEADS = 8
GDN_DK = 128
GDN_DV = 128
GDN_CONV = 5
GDN_CHUNK = 64
GDN_SUPER = 256
N_BRANCHES = 3
RMS_EPS = 1e-6

NA_W = NA_HEADS * HEAD_DIM
DA_QK_W = DA_HEADS * 2 * DA_QK_DIM
DA_V_W = DA_HEADS * DA_V_DIM
GDN_K_W = GDN_HEADS * GDN_DK
GDN_V_W = GDN_HEADS * GDN_DV

LANES = 128
VMEM_LIMIT = 56 * 1024 * 1024
NEG_BIG = -1e30

_LOG2_LANES = LANES.bit_length() - 1
_LOG2_CHUNK = GDN_CHUNK.bit_length() - 1

_NT = (((1,), (1,)), ((), ()))
_TN = (((0,), (0,)), ((), ()))


def _cparams(sem):
    return pltpu.CompilerParams(dimension_semantics=sem, vmem_limit_bytes=VMEM_LIMIT)


def _sigmoid(x):
    return 1.0 / (1.0 + jnp.exp(-x))


def _dot(a, b):
    return jnp.dot(a.astype(BF16), b.astype(BF16), preferred_element_type=F32)


def _dot_nt(a, b):
    return lax.dot_general(a.astype(BF16), b.astype(BF16), _NT, preferred_element_type=F32)


def _mod_kernel(c_ref, w_ref, b_ref, o_ref):
    c = c_ref[...]
    a = c * _sigmoid(c)
    o_ref[0] = jnp.dot(a, w_ref[0], preferred_element_type=F32,
                       precision=lax.Precision.HIGHEST) + b_ref[0]


def _adaln(c, ada_w, ada_b):
    depth, d, n = ada_w.shape
    b = c.shape[0]
    rows = 8
    cp = jnp.zeros((rows, d), F32).at[:b].set(c)
    tn = 1024
    out = pl.pallas_call(
        _mod_kernel,
        grid=(depth, n // tn),
        in_specs=[pl.BlockSpec((rows, d), lambda l, j: (0, 0)),
                  pl.BlockSpec((1, d, tn), lambda l, j: (l, 0, j)),
                  pl.BlockSpec((1, 1, tn), lambda l, j: (l, 0, j))],
        out_specs=pl.BlockSpec((1, rows, tn), lambda l, j: (l, 0, j)),
        out_shape=jax.ShapeDtypeStruct((depth, rows, n), F32),
        compiler_params=_cparams(("arbitrary", "arbitrary")),
        name="adaln_mod",
    )(cp, ada_w, ada_b.reshape(depth, 1, n))
    return out[:, :b]


def _norm_kernel(x_ref, g_ref, sc_ref, sh_ref, o_ref):
    x = x_ref[0]
    y = x * lax.rsqrt(jnp.mean(x * x, axis=-1, keepdims=True) + RMS_EPS) * g_ref[...]
    o_ref[0] = (y * (1.0 + sc_ref[0]) + sh_ref[0]).astype(o_ref.dtype)


def _norm_mod(x, g, sc, sh):
    b, s, d = x.shape
    ts = min(512, s)
    return pl.pallas_call(
        _norm_kernel,
        grid=(b, s // ts),
        in_specs=[pl.BlockSpec((1, ts, d), lambda i, j: (i, j, 0)),
                  pl.BlockSpec((1, d), lambda i, j: (0, 0)),
                  pl.BlockSpec((1, 1, d), lambda i, j: (i, 0, 0)),
                  pl.BlockSpec((1, 1, d), lambda i, j: (i, 0, 0))],
        out_specs=pl.BlockSpec((1, ts, d), lambda i, j: (i, j, 0)),
        out_shape=jax.ShapeDtypeStruct((b, s, d), BF16),
        compiler_params=_cparams(("parallel", "parallel")),
        name="norm_mod",
    )(x, g.reshape(1, d), sc.reshape(b, 1, d), sh.reshape(b, 1, d))


def _group_rms_store(acc, gain_ref, o_ref, group):
    tn = acc.shape[1]
    for s in range(tn // LANES):
        sl = slice(s * LANES, (s + 1) * LANES)
        a = acc[:, sl]
        sq = a * a
        if group == LANES:
            ms = jnp.sum(sq, axis=-1, keepdims=True) * (1.0 / LANES)
        else:
            lo = lax.broadcasted_iota(jnp.int32, a.shape, 1) < group
            s_lo = jnp.sum(jnp.where(lo, sq, 0.0), axis=-1, keepdims=True)
            s_hi = jnp.sum(jnp.where(lo, 0.0, sq), axis=-1, keepdims=True)
            ms = jnp.where(lo, s_lo, s_hi) * (1.0 / group)
        y = a * lax.rsqrt(ms + RMS_EPS) * gain_ref[:, sl]
        o_ref[:, sl] = y.astype(o_ref.dtype)


def _proj_kernel(h_ref, w_ref, *rest, group):
    acc = jnp.dot(h_ref[...], w_ref[...], preferred_element_type=F32)
    if group is None:
        (o_ref,) = rest
        o_ref[...] = acc.astype(o_ref.dtype)
    else:
        gain_ref, o_ref = rest
        _group_rms_store(acc, gain_ref, o_ref, group)


def _proj(h2d, w, out_dtype, group=None, gain=None):
    t, d = h2d.shape
    n = w.shape[1]
    tm = min(1024, t)
    tn = min(512, n)
    in_specs = [pl.BlockSpec((tm, d), lambda i, j: (i, 0)),
                pl.BlockSpec((d, tn), lambda i, j: (0, j))]
    args = [h2d, w.astype(BF16)]
    if group is not None:
        assert group in (LANES // 2, LANES)
        in_specs.append(pl.BlockSpec((1, tn), lambda i, j: (0, j)))
        args.append(gain.reshape(1, n).astype(F32))
    return pl.pallas_call(
        functools.partial(_proj_kernel, group=group),
        grid=(t // tm, n // tn),
        in_specs=in_specs,
        out_specs=pl.BlockSpec((tm, tn), lambda i, j: (i, j)),
        out_shape=jax.ShapeDtypeStruct((t, n), out_dtype),
        compiler_params=_cparams(("parallel", "arbitrary")),
        name="proj",
    )(*args)


PROJ_TN = 512


def _proj_mixers_kernel(h_ref, w_ref, gain_ref, attn_ref, gdn_ref, *, roles):
    j = pl.program_id(1)
    acc = jnp.dot(h_ref[...], w_ref[...], preferred_element_type=F32)

    def tiles_with(role):
        cond = None
        for n, r in enumerate(roles):
            if r == role:
                cond = (j == n) if cond is None else (cond | (j == n))
        return cond

    @pl.when(tiles_with("rms128"))
    def _():
        _group_rms_store(acc, gain_ref, attn_ref, LANES)

    @pl.when(tiles_with("rms64"))
    def _():
        _group_rms_store(acc, gain_ref, attn_ref, LANES // 2)

    @pl.when(tiles_with("bf16"))
    def _():
        attn_ref[...] = acc.astype(attn_ref.dtype)

    @pl.when(tiles_with("f32"))
    def _():
        gdn_ref[...] = acc


def _proj_mixers(h2d, w, gain, roles):
    t, d = h2d.shape
    n = w.shape[1]
    tm = min(1024, t)
    tn = PROJ_TN
    n_attn = sum(r != "f32" for r in roles)
    assert len(roles) * tn == n and all(r == "f32" for r in roles[n_attn:])
    return pl.pallas_call(
        functools.partial(_proj_mixers_kernel, roles=tuple(roles)),
        grid=(t // tm, len(roles)),
        in_specs=[pl.BlockSpec((tm, d), lambda i, j: (i, 0)),
                  pl.BlockSpec((d, tn), lambda i, j: (0, j)),
                  pl.BlockSpec((1, tn), lambda i, j: (0, j))],
        out_specs=[pl.BlockSpec((tm, tn), lambda i, j: (i, jnp.minimum(j, n_attn - 1))),
                   pl.BlockSpec((tm, tn), lambda i, j: (i, jnp.maximum(j - n_attn, 0)))],
        out_shape=[jax.ShapeDtypeStruct((t, n_attn * tn), BF16),
                   jax.ShapeDtypeStruct((t, n - n_attn * tn), F32)],
        compiler_params=_cparams(("parallel", "arbitrary")),
        name="proj_mixers",
    )(h2d, w.astype(BF16), gain.reshape(1, n).astype(F32))


NA_ROWS_PER_STEP = 8
NA_BAND = NA_WIN_R * GRID_W


def _na_bias_table(rpb):
    h = rpb.shape[0]
    w = jnp.arange(GRID_W)
    col_start = jnp.clip(w - NA_WIN_C // 2, 0, GRID_W - NA_WIN_C)
    ck = jnp.arange(GRID_W)
    inwin = (ck[None, :] >= col_start[:, None]) & (ck[None, :] < col_start[:, None] + NA_WIN_C)
    cbi = jnp.clip(ck[None, :] - w[:, None] + (NA_WIN_C - 1), 0, 2 * NA_WIN_C - 2)
    e = jnp.arange(NA_WIN_R)
    j = jnp.arange(NA_WIN_R)
    rbi = j[None, :] + (NA_WIN_R - 1) - e[:, None]
    tab = rpb[:, rbi]
    tab = tab[:, :, :, cbi]
    tab = jnp.where(inwin[None, None, None], tab, NEG_BIG)
    return tab.transpose(1, 0, 3, 2, 4).reshape(NA_WIN_R, h, GRID_W, NA_BAND).astype(F32)


def _na_kernel(q_ref, kp_ref, kc_ref, kn_ref, vp_ref, vc_ref, vn_ref, tab_ref, o_ref,
               kbuf, vbuf, *, rows):
    rb = pl.program_id(1)
    blk = NA_ROWS_PER_STEP * GRID_W
    for n, (kr, vr) in enumerate(((kp_ref, vp_ref), (kc_ref, vc_ref), (kn_ref, vn_ref))):
        kbuf[n * blk:(n + 1) * blk, :] = kr[0]
        vbuf[n * blk:(n + 1) * blk, :] = vr[0]
    heads = [slice(h * HEAD_DIM, (h + 1) * HEAD_DIM) for h in range(NA_HEADS)]

    def band(i):
        r = rb * NA_ROWS_PER_STEP + i
        r_start = jnp.clip(r - NA_WIN_R // 2, 0, rows - NA_WIN_R)
        loc = pl.multiple_of((r_start - (rb - 1) * NA_ROWS_PER_STEP) * GRID_W, GRID_W)
        return r - r_start, loc, slice(i * GRID_W, (i + 1) * GRID_W)

    def scores(i):
        e, loc, rs = band(i)
        return [lax.dot_general(q_ref[0, rs, hs], kbuf[pl.ds(loc, NA_BAND), hs], _NT,
                                preferred_element_type=F32) + tab_ref[e, h]
                for h, hs in enumerate(heads)]

    def finish(i, s_heads):
        _, loc, rs = band(i)
        for hs, s in zip(heads, s_heads):
            p = jnp.exp(s - jnp.max(s, axis=-1, keepdims=True))
            l = jnp.sum(p, axis=-1, keepdims=True)
            o = jnp.dot(p.astype(BF16), vbuf[pl.ds(loc, NA_BAND), hs], preferred_element_type=F32) / l
            o_ref[0, rs, hs] = o.astype(o_ref.dtype)

    pending = scores(0)
    for i in range(1, NA_ROWS_PER_STEP):
        nxt = scores(i)
        finish(i - 1, pending)
        pending = nxt
    finish(NA_ROWS_PER_STEP - 1, pending)


def _natten(qkv, rpb):
    b, s, _ = qkv.shape
    rows = s // GRID_W
    blk = NA_ROWS_PER_STEP * GRID_W
    nrb = s // blk
    tab = _na_bias_table(rpb)

    def at(off, col):
        return lambda i, j: (i, jnp.clip(j + off, 0, nrb - 1), col)

    return pl.pallas_call(
        functools.partial(_na_kernel, rows=rows),
        grid=(b, nrb),
        in_specs=[pl.BlockSpec((1, blk, NA_W), at(0, 0)),
                  pl.BlockSpec((1, blk, NA_W), at(-1, 1)),
                  pl.BlockSpec((1, blk, NA_W), at(0, 1)),
                  pl.BlockSpec((1, blk, NA_W), at(1, 1)),
                  pl.BlockSpec((1, blk, NA_W), at(-1, 2)),
                  pl.BlockSpec((1, blk, NA_W), at(0, 2)),
                  pl.BlockSpec((1, blk, NA_W), at(1, 2)),
                  pl.BlockSpec(tab.shape, lambda i, j: (0, 0, 0, 0))],
        out_specs=pl.BlockSpec((1, blk, NA_W), at(0, 0)),
        out_shape=jax.ShapeDtypeStruct((b, s, NA_W), BF16),
        scratch_shapes=[pltpu.VMEM((3 * blk, NA_W), BF16), pltpu.VMEM((3 * blk, NA_W), BF16)],
        compiler_params=_cparams(("parallel", "parallel")),
        name="natten",
    )(*([qkv] * 7), tab)


DA_TILE = 512
DA_BLOCK = 256
DA_ONES_ROWS = 16
LOG2E = math.log2(math.e)


def _da_slopes():
    return [LOG2E * 2.0 ** (-8.0 * (i + 1) / DA_HEADS) for i in range(DA_HEADS)]


def _bf16_parts(x):
    p1 = x.astype(BF16)
    r1 = x - p1.astype(F32)
    p2 = r1.astype(BF16)
    p3 = (r1 - p2.astype(F32)).astype(BF16)
    return [p1, p2, p3]


def _da_pos_tables(tile):
    slopes = jnp.asarray(_da_slopes(), F32)[:, None]
    pos = jnp.arange(tile)
    parts = _bf16_parts(slopes * pos.astype(F32)[None, :])
    ones = [jnp.ones((DA_HEADS, tile), BF16)] * 3
    zpad = jnp.zeros((DA_HEADS, DA_QK_DIM - 6, tile), BF16)
    kfeat = jnp.concatenate([jnp.stack(parts + ones, axis=1), zpad], axis=1)
    qfeat = jnp.concatenate([jnp.stack(ones + [-p for p in parts], axis=1), zpad], axis=1)
    dist = jnp.abs(pos[:, None] - pos[None, :]).astype(F32)
    bdiag = -slopes[:, :, None] * dist[None]
    return qfeat, kfeat, bdiag


def _da_kernel(qt_ref, ka_ref, vt_ref, qpos_ref, ones_ref, bdiag_ref, slope_ref, lam_ref, g_ref,
               o_ref, qa_sc, st0_sc, st1_sc, mt0_sc, mt1_sc, m_sc, acc_sc, *, tile, lam_init):
    hd = pl.program_id(1)
    qi = pl.program_id(2)
    nk = vt_ref.shape[2]
    nv = nk - 1
    half = DA_QK_DIM

    qt = qt_ref[0]
    qpos = qpos_ref[0]
    for sel, pp in enumerate((qpos, -qpos, jnp.zeros_like(qpos))):
        qa_sc[sel, 0] = jnp.concatenate([qt[0:half], pp], axis=0)
        qa_sc[sel, 1] = jnp.concatenate([pp, qt[half:2 * half]], axis=0)
    slope = slope_ref[hd]
    ones_rows = ones_ref[...]

    def softmax_pv(c, st, mt, shift, vat):
        m_old = m_sc[c]
        m_new = jnp.maximum(m_old, mt + shift)
        alpha = jnp.exp2(m_old - m_new)
        pt = jnp.exp2(st - (m_new - shift)).astype(BF16)
        acc_sc[c] = alpha * acc_sc[c] + jnp.dot(vat, pt, preferred_element_type=F32)
        m_sc[c] = m_new

    m_sc[...] = jnp.full(m_sc.shape, NEG_BIG, F32)
    acc_sc[...] = jnp.zeros(acc_sc.shape, F32)
    vat = jnp.concatenate([vt_ref[0, 0, qi], ones_rows], axis=0)
    start = pl.multiple_of(qi * tile, tile)
    for c in range(2):
        ka = ka_ref[0, 0, c, pl.ds(start, tile), :]
        st = jnp.dot(ka, qa_sc[2, c], preferred_element_type=F32) + bdiag_ref[0]
        softmax_pv(c, st, jnp.max(st, axis=0, keepdims=True), 0.0, vat)

    if nv == 0:
        pass
    else:
        blocks = [slice(i, i + DA_BLOCK) for i in range(0, tile, DA_BLOCK)]

        def key_tile(v):
            after = (v >= qi).astype(jnp.int32)
            return v + after, after

        def score_units(v, st_sc, mt_sc):
            t, after = key_tile(v)
            for c in range(2):
                for qs in blocks:
                    mcol = None
                    for ks in blocks:
                        start = pl.multiple_of(t * tile + ks.start, DA_BLOCK)
                        ka = ka_ref[0, 0, c, pl.ds(start, DA_BLOCK), :]
                        st = jnp.dot(ka, qa_sc[after, c, :, qs], preferred_element_type=F32)
                        st_sc[c, ks, qs] = st
                        mk = jnp.max(st, axis=0, keepdims=True)
                        mcol = mk if mcol is None else jnp.maximum(mcol, mk)
                        if ks is blocks[-1]:
                            mt_sc[c, :, qs] = mcol
                        yield

        def softmax_units(v, st_sc, mt_sc):
            t, _ = key_tile(v)
            shift = -slope * (tile * jnp.abs(t - qi)).astype(F32)
            vat = jnp.concatenate([vt_ref[0, 0, t], ones_rows], axis=0)
            stats = []
            for c in range(2):
                m_old = m_sc[c]
                m_new = jnp.maximum(m_old, mt_sc[c] + shift)
                stats.append((jnp.exp2(m_old - m_new), m_new - shift))
                m_sc[c] = m_new
            for c, (alpha, msub) in enumerate(stats):
                for qs in blocks:
                    acc = alpha[:, qs] * acc_sc[c, :, qs]
                    for ks in blocks:
                        pt = jnp.exp2(st_sc[c, ks, qs] - msub[:, qs]).astype(BF16)
                        acc = acc + jnp.dot(vat[:, ks], pt, preferred_element_type=F32)
                        if ks is blocks[-1]:
                            acc_sc[c, :, qs] = acc
                        yield

        def overlapped(scores, softmax):
            if scores is not None:
                next(scores)
            for _ in softmax:
                if scores is not None:
                    next(scores, None)

        bufs = ((st0_sc, mt0_sc), (st1_sc, mt1_sc))
        overlapped(None, score_units(0, *bufs[0]))

        def body(v, carry):
            for parity in range(2):
                @pl.when((v & 1) == parity)
                def _():
                    overlapped(score_units(v + 1, *bufs[1 - parity]), softmax_units(v, *bufs[parity]))
            return carry

        lax.fori_loop(0, nv - 1, body, 0)
        overlapped(None, softmax_units(nv - 1, *bufs[(nv - 1) % 2]))

    lp = lam_ref[...]
    lam = (jnp.exp(jnp.sum(lp[0:1] * lp[1:2], axis=-1, keepdims=True))
           - jnp.exp(jnp.sum(lp[2:3] * lp[3:4], axis=-1, keepdims=True)) + lam_init)
    a0 = acc_sc[0]
    a1 = acc_sc[1]
    o0 = a0[0:DA_V_DIM] / a0[DA_V_DIM:DA_V_DIM + 1]
    o1 = a1[0:DA_V_DIM] / a1[DA_V_DIM:DA_V_DIM + 1]
    a = o0 - lam * o1
    y = a * lax.rsqrt(jnp.mean(a * a, axis=0, keepdims=True) + RMS_EPS) * g_ref[...]
    o_ref[0] = (y * (1.0 - lam_init)).T.astype(o_ref.dtype)


def _diff_attn(qk, v, lam_p, subln_g, lam_init):
    b, s, _ = v.shape
    tile = min(DA_TILE, s)
    nt = s // tile
    nh = DA_HEADS
    qfeat, kfeat, bdiag = _da_pos_tables(tile)
    slopes = jnp.asarray(_da_slopes(), F32)

    qt = qk[:, :, :DA_QK_W].transpose(0, 2, 1)
    k4 = qk[:, :, DA_QK_W:].reshape(b, s, nh, 2, DA_QK_DIM)
    kp = jnp.broadcast_to(jnp.tile(kfeat.transpose(0, 2, 1), (1, nt, 1))[None], (b, nh, s, DA_QK_DIM))
    k0 = jnp.concatenate([k4[:, :, :, 0].transpose(0, 2, 1, 3), kp], axis=-1)
    k1 = jnp.concatenate([kp, k4[:, :, :, 1].transpose(0, 2, 1, 3)], axis=-1)
    ka = jnp.stack([k0, k1], axis=2)
    vt = v.reshape(b, nt, tile, nh, DA_V_DIM).transpose(0, 3, 1, 4, 2)
    ones_rows = jnp.zeros((DA_ONES_ROWS, tile), BF16).at[0].set(1.0)

    return pl.pallas_call(
        functools.partial(_da_kernel, tile=tile, lam_init=lam_init),
        grid=(b, nh, nt),
        in_specs=[pl.BlockSpec((1, LANES, tile), lambda i, h, q: (i, h, q)),
                  pl.BlockSpec((1, 1, 2, s, LANES), lambda i, h, q: (i, h, 0, 0, 0)),
                  pl.BlockSpec((1, 1, nt, DA_V_DIM, tile), lambda i, h, q: (i, h, 0, 0, 0)),
                  pl.BlockSpec((1, DA_QK_DIM, tile), lambda i, h, q: (h, 0, 0)),
                  pl.BlockSpec((DA_ONES_ROWS, tile), lambda i, h, q: (0, 0)),
                  pl.BlockSpec((1, tile, tile), lambda i, h, q: (h, 0, 0)),
                  pl.BlockSpec(memory_space=pltpu.SMEM),
                  pl.BlockSpec((4, DA_QK_DIM), lambda i, h, q: (0, 0)),
                  pl.BlockSpec((DA_V_DIM, 1), lambda i, h, q: (0, 0))],
        out_specs=pl.BlockSpec((1, tile, DA_V_DIM), lambda i, h, q: (i, q, h)),
        out_shape=jax.ShapeDtypeStruct((b, s, DA_V_W), BF16),
        scratch_shapes=[pltpu.VMEM((3, 2, LANES, tile), BF16),
                        pltpu.VMEM((2, tile, tile), F32),
                        pltpu.VMEM((2, tile, tile), F32),
                        pltpu.VMEM((2, 1, tile), F32),
                        pltpu.VMEM((2, 1, tile), F32),
                        pltpu.VMEM((2, 1, tile), F32),
                        pltpu.VMEM((2, DA_V_DIM + DA_ONES_ROWS, tile), F32)],
        compiler_params=_cparams(("parallel", "parallel", "arbitrary")),
        name="diff_attn",
    )(qt, ka, vt, qfeat, ones_rows, bdiag, slopes, lam_p.astype(F32),
      subln_g.reshape(DA_V_DIM, 1).astype(F32))


def _gdn_gate_kernel(lg_ref, alog_ref, dt_ref, beta_ref, gc_ref, gt_ref):
    lg = lg_ref[0]
    nh = 2 * GDN_HEADS
    beta_ref[0] = _sigmoid(lg[0:nh])
    z = lg[nh:2 * nh] + dt_ref[...]
    sp = jnp.maximum(z, 0.0) + jnp.log(1.0 + jnp.exp(-jnp.abs(z)))
    g = -jnp.exp(alog_ref[...]) * sp
    ts = g.shape[1]
    pos = lax.broadcasted_iota(jnp.int32, g.shape, 1) & (GDN_CHUNK - 1)
    pre = g
    suf = g
    sh = 1
    while sh < GDN_CHUNK:
        pre = pre + jnp.where(pos >= sh, pltpu.roll(pre, sh, axis=1), 0.0)
        suf = suf + jnp.where(pos < GDN_CHUNK - sh, pltpu.roll(suf, ts - sh, axis=1), 0.0)
        sh *= 2
    bwd = lax.broadcasted_iota(jnp.int32, g.shape, 0) >= GDN_HEADS
    gc_ref[0] = jnp.where(bwd, suf, pre)
    gt_ref[0] = jnp.where(bwd, pre, suf) - g


def _gdn_gates(lgt, a_log, dt_bias):
    b, _, s = lgt.shape
    nh = 2 * GDN_HEADS
    ts = min(2048, s)
    shp = jax.ShapeDtypeStruct((b, nh, s), F32)
    spec = pl.BlockSpec((1, nh, ts), lambda i, j: (i, 0, j))
    return pl.pallas_call(
        _gdn_gate_kernel,
        grid=(b, s // ts),
        in_specs=[pl.BlockSpec((1, 2 * nh, ts), lambda i, j: (i, 0, j)),
                  pl.BlockSpec((nh, 1), lambda i, j: (0, 0)),
                  pl.BlockSpec((nh, 1), lambda i, j: (0, 0))],
        out_specs=[spec, spec, spec],
        out_shape=[shp, shp, shp],
        compiler_params=_cparams(("parallel", "parallel")),
        name="gdn_gates",
    )(lgt, a_log.reshape(nh, 1).astype(F32), dt_bias.reshape(nh, 1).astype(F32))


GDN_HALO = 8


def _gdn_conv_kernel(x_ref, p_ref, n_ref, w_ref, o_ref, ext, *, ts):
    cg = pl.program_id(0)
    i = pl.program_id(2)
    last = pl.num_programs(2) - 1
    ext[0:GDN_HALO, :] = jnp.where(i > 0, p_ref[0], 0.0)
    ext[GDN_HALO:GDN_HALO + ts, :] = x_ref[0]
    ext[GDN_HALO + ts:2 * GDN_HALO + ts, :] = jnp.where(i < last, n_ref[0], 0.0)
    y = None
    for tap in range(GDN_CONV):
        off = GDN_HALO + tap - GDN_CONV // 2
        term = ext[off:off + ts, :] * w_ref[0, tap:tap + 1, :]
        y = term if y is None else y + term
    y = y * _sigmoid(y)
    qscale = jnp.where(cg == 0, GDN_DK ** -0.5, 1.0).astype(F32)
    is_v = cg == 2
    for h in range(GDN_HEADS):
        hs = slice(h * GDN_DK, (h + 1) * GDN_DK)
        a = y[:, hs]
        nrm = lax.rsqrt(jnp.sum(a * a, axis=-1, keepdims=True) + RMS_EPS) * qscale
        o_ref[0, 0, :, hs] = a * jnp.where(is_v, 1.0, nrm)


def _gdn_conv(u, conv_w):
    b, s, _ = u.shape
    cw = GDN_K_W
    ts = min(512, s)
    nb = s // GDN_HALO
    per = ts // GDN_HALO
    w3 = conv_w.reshape(GDN_CONV, 3, cw).transpose(1, 0, 2)
    w3 = jnp.pad(w3, ((0, 0), (0, 8 - GDN_CONV), (0, 0))).astype(F32)
    return pl.pallas_call(
        functools.partial(_gdn_conv_kernel, ts=ts),
        grid=(3, b, s // ts),
        in_specs=[pl.BlockSpec((1, ts, cw), lambda c, i, j: (i, j, c)),
                  pl.BlockSpec((1, GDN_HALO, cw),
                               lambda c, i, j: (i, jnp.maximum(j * per - 1, 0), c)),
                  pl.BlockSpec((1, GDN_HALO, cw),
                               lambda c, i, j: (i, jnp.minimum((j + 1) * per, nb - 1), c)),
                  pl.BlockSpec((1, 8, cw), lambda c, i, j: (c, 0, 0))],
        out_specs=pl.BlockSpec((1, 1, ts, cw), lambda c, i, j: (c, i, j, 0)),
        out_shape=jax.ShapeDtypeStruct((3, b, s, cw), F32),
        scratch_shapes=[pltpu.VMEM((ts + 2 * GDN_HALO, cw), F32)],
        compiler_params=_cparams(("parallel", "parallel", "parallel")),
        name="gdn_conv",
    )(u, u, u, w3)


class _GdnChain(NamedTuple):
    q: jax.Array
    k: jax.Array
    v: jax.Array
    cols: jax.Array
    gr: jax.Array
    rev: bool
    state: object
    out: object


def _gdn_chains(chains):
    sc = GDN_SUPER
    ck = GDN_CHUNK
    nchunk = sc // ck
    row = lax.broadcasted_iota(jnp.int32, (ck, sc), 0)
    lane = lax.broadcasted_iota(jnp.int32, (ck, sc), 1)
    col = lane & (ck - 1)
    lane_chunk = lane >> _LOG2_CHUNK
    eye = jnp.where(row == col, 1.0, 0.0).astype(F32)
    masks = {False: (row >= col, row > col), True: (row <= col, row < col)}
    ii = lax.broadcasted_iota(jnp.int32, (sc, sc), 0)
    jj = lax.broadcasted_iota(jnp.int32, (sc, sc), 1)
    bd_mask = jnp.where((ii >> _LOG2_CHUNK) == (jj >> _LOG2_CHUNK), 1.0, 0.0).astype(BF16)

    def pack(full):
        out = full[0:ck]
        for b in range(1, nchunk):
            out = jnp.where(lane_chunk == b, full[b * ck:(b + 1) * ck], out)
        return out

    def blockdiag(packed_bf16):
        return jnp.concatenate([packed_bf16] * nchunk, axis=0) * bd_mask

    def lanes(c):
        return jnp.broadcast_to(c, (sc, LANES))

    bcast = [(lanes(ch.cols[:, 0:1]), lanes(ch.cols[:, 1:2]), lanes(ch.cols[:, 2:3])) for ch in chains]

    decays, kbs, a_mats = [], [], []
    for ch, (beta, gc, _) in zip(chains, bcast):
        incl, strict = masks[ch.rev]
        gc2 = jnp.concatenate([gc, gc], axis=1)
        decay = jnp.exp(jnp.where(incl, pack(gc2) - ch.gr, NEG_BIG))
        kb = ch.k * beta
        decays.append(decay)
        kbs.append(kb)
        a_mats.append(jnp.where(strict, pack(_dot_nt(kb, ch.k)) * decay, 0.0))

    xs = [-a for a in a_mats]
    ts = [eye + x for x in xs]
    xs = [_dot(x, blockdiag(x.astype(BF16))) for x in xs]
    span = 2
    while span < ck:
        last = 2 * span >= ck
        new_ts, new_xs = [], []
        for t, x in zip(ts, xs):
            w_bd = blockdiag(x.astype(BF16))
            if last:
                new_ts.append(t + _dot(t, w_bd))
                new_xs.append(None)
            else:
                both = _dot(jnp.concatenate([t, x], axis=0), w_bd)
                new_ts.append(t + both[0:ck])
                new_xs.append(both[ck:2 * ck])
        ts, xs = new_ts, new_xs
        span *= 2
    resids = []
    for a, t in zip(a_mats, ts):
        a_hi = a.astype(BF16)
        a_lo = (a - a_hi.astype(F32)).astype(BF16)
        t_hi = t.astype(BF16)
        t_lo = (t - t_hi.astype(F32)).astype(BF16)
        hi = jnp.dot(jnp.concatenate([a_hi, a_lo], axis=0), blockdiag(t_hi), preferred_element_type=F32)
        a_t = hi[0:ck] + hi[ck:2 * ck] + jnp.dot(a_hi, blockdiag(t_lo), preferred_element_type=F32)
        resids.append(eye - t - a_t)
    ts = [t + _dot(t, blockdiag(r.astype(BF16))) for t, r in zip(ts, resids)]

    pre = []
    for ch, (beta, gc, gt), decay, kb, t in zip(chains, bcast, decays, kbs, ts):
        egc = jnp.exp(gc)
        uw = _dot(blockdiag(t.astype(BF16)), jnp.concatenate([ch.v * beta, kb * egc], axis=1))
        pre.append(dict(u=uw[:, 0:GDN_DV], w=uw[:, GDN_DV:], qg=ch.q * egc,
                        ai=pack(_dot_nt(ch.q, ch.k)) * decay, kt_t=(ch.k * jnp.exp(gt)).T,
                        gl=jnp.exp(gc + gt)))

    states = [ch.state[...] for ch in chains]
    outs = [[None] * nchunk for _ in chains]
    for step in range(nchunk):
        order = [nchunk - 1 - step if ch.rev else step for ch in chains]
        spans = [slice(j * ck, (j + 1) * ck) for j in order]
        from_state = [_dot(jnp.concatenate([p["w"][sl], p["qg"][sl]], axis=0), s_mat)
                      for p, sl, s_mat in zip(pre, spans, states)]
        v_news = [p["u"][sl] - r[0:ck] for p, sl, r in zip(pre, spans, from_state)]
        from_v = [_dot(jnp.concatenate([p["ai"][:, sl], p["kt_t"][:, sl]], axis=0), v_new)
                  for p, sl, v_new in zip(pre, spans, v_news)]
        for n, (p, j, r1, r2) in enumerate(zip(pre, order, from_state, from_v)):
            outs[n][j] = r1[ck:2 * ck] + r2[0:ck]
            states[n] = states[n] * p["gl"][j * ck:j * ck + 1, :] + r2[ck:]
    for ch, s_mat, o in zip(chains, states, outs):
        ch.state[...] = s_mat
        ch.out[...] = jnp.concatenate(o, axis=0)


GDN_HEADS_PER_STEP = 8


def _gdn_scan_kernel(qf_ref, kf_ref, vf_ref, cf_ref, rf_ref, qb_ref, kb_ref, vb_ref, cb_ref, rb_ref,
                     of_ref, ob_ref, state):
    hg = pl.program_id(1)

    @pl.when(pl.program_id(2) == 0)
    def _():
        state[...] = jnp.zeros(state.shape, F32)

    per_group = 3 * GDN_HEADS_PER_STEP
    shift = (LANES - per_group * hg) & (LANES - 1)
    cols_fb = (pltpu.roll(cf_ref[0, 0], shift, axis=1), pltpu.roll(cb_ref[0, 0], shift, axis=1))

    chains = []
    for i in range(GDN_HEADS_PER_STEP):
        hd = hg * GDN_HEADS_PER_STEP + i
        ks = slice(i * GDN_DK, (i + 1) * GDN_DK)
        vs = slice(i * GDN_DV, (i + 1) * GDN_DV)
        for rev, (q_ref, k_ref, v_ref, r_ref, o_ref) in enumerate(
                ((qf_ref, kf_ref, vf_ref, rf_ref, of_ref), (qb_ref, kb_ref, vb_ref, rb_ref, ob_ref))):
            chains.append(_GdnChain(
                q=q_ref[0, 0, :, ks], k=k_ref[0, 0, :, ks], v=v_ref[0, 0, :, vs],
                cols=cols_fb[rev][:, 3 * i:3 * i + 3],
                gr=r_ref[0, 0, pl.ds(hd, 1), :], rev=bool(rev),
                state=state.at[2 * i + rev], out=o_ref.at[0, :, vs]))
    _gdn_chains(chains)


def _gdn_scan(qkv, cols, rows):
    _, b, s, _ = qkv.shape
    sc = GDN_SUPER
    nsc = s // sc
    hps = GDN_HEADS_PER_STEP

    def specs(d):
        seq = (lambda n: nsc - 1 - n) if d else (lambda n: n)
        return [pl.BlockSpec((1, 1, sc, hps * GDN_DK), lambda i, h, n: (0, i, seq(n), h)),
                pl.BlockSpec((1, 1, sc, hps * GDN_DK), lambda i, h, n: (1, i, seq(n), h)),
                pl.BlockSpec((1, 1, sc, hps * GDN_DV), lambda i, h, n: (2, i, seq(n), h)),
                pl.BlockSpec((1, 1, sc, LANES), lambda i, h, n: (i, d, seq(n), 0)),
                pl.BlockSpec((1, 1, GDN_HEADS, sc), lambda i, h, n: (i, d, 0, seq(n)))]

    out_shape = jax.ShapeDtypeStruct((b, s, GDN_V_W), F32)
    return pl.pallas_call(
        _gdn_scan_kernel,
        grid=(b, GDN_HEADS // hps, nsc),
        in_specs=specs(0) + specs(1),
        out_specs=[pl.BlockSpec((1, sc, hps * GDN_DV), lambda i, h, n: (i, n, h)),
                   pl.BlockSpec((1, sc, hps * GDN_DV), lambda i, h, n: (i, nsc - 1 - n, h))],
        out_shape=[out_shape, out_shape],
        scratch_shapes=[pltpu.VMEM((2 * hps, GDN_DK, GDN_DV), F32)],
        compiler_params=_cparams(("parallel", "parallel", "arbitrary")),
        name="gdn_scan",
    )(qkv, qkv, qkv, cols, rows, qkv, qkv, qkv, cols, rows)


def _gdn_out_kernel(of_ref, ob_ref, z_ref, g_ref, o_ref):
    o = of_ref[0] + ob_ref[0]
    z = z_ref[0].astype(F32)
    gate = z * _sigmoid(z)
    for h in range(GDN_HEADS):
        hs = slice(h * GDN_DV, (h + 1) * GDN_DV)
        a = o[:, hs]
        y = a * lax.rsqrt(jnp.mean(a * a, axis=-1, keepdims=True) + RMS_EPS) * g_ref[...]
        o_ref[0, :, hs] = (y * gate[:, hs]).astype(o_ref.dtype)


def _gdn_out(o_f, o_b, u, norm_g):
    b, s, w = o_f.shape
    ts = min(512, s)
    spec = pl.BlockSpec((1, ts, w), lambda i, j: (i, j, 0))
    z_spec = pl.BlockSpec((1, ts, w), lambda i, j: (i, j, 3))
    return pl.pallas_call(
        _gdn_out_kernel,
        grid=(b, s // ts),
        in_specs=[spec, spec, z_spec, pl.BlockSpec((1, GDN_DV), lambda i, j: (0, 0))],
        out_specs=spec,
        out_shape=jax.ShapeDtypeStruct((b, s, w), BF16),
        compiler_params=_cparams(("parallel", "parallel")),
        name="gdn_out",
    )(o_f, o_b, u, norm_g.reshape(1, GDN_DV).astype(F32))


def _gated_deltanet(u, lg, conv_w, a_log, dt_bias, norm_g):
    b, s, _ = u.shape
    nh = GDN_HEADS
    lgt = lg[:, :, :4 * nh].transpose(0, 2, 1)
    beta, gc, gt = _gdn_gates(lgt, a_log, dt_bias)
    rows = gc.reshape(b, 2, nh, s)
    cols = jnp.stack([beta, gc, gt], axis=-1).reshape(b, 2, nh, s, 3)
    cols = cols.transpose(0, 1, 3, 2, 4).reshape(b, 2, s, 3 * nh)
    cols = jnp.pad(cols, ((0, 0), (0, 0), (0, 0), (0, LANES - 3 * nh)))
    qkv = _gdn_conv(u, conv_w)
    o_f, o_b = _gdn_scan(qkv, cols, rows)
    return _gdn_out(o_f, o_b, u, norm_g)


def _merge_kernel(h_ref, ya_ref, yb_ref, yc_ref, gw0_ref, gw1_ref, gw2_ref,
                  gb0_ref, gb1_ref, gb2_ref, wa_ref, wb_ref, wc_ref, wo_ref,
                  x_ref, gt_ref, o_ref, acc):
    j = pl.program_id(1)
    h = h_ref[...]

    def gate(gw_ref, gb_ref):
        return _sigmoid(jnp.dot(h, gw_ref[...], preferred_element_type=F32) + gb_ref[...])

    m = (gate(gw0_ref, gb0_ref) * jnp.dot(ya_ref[...], wa_ref[...], preferred_element_type=F32)
         + gate(gw1_ref, gb1_ref) * jnp.dot(yb_ref[...], wb_ref[...], preferred_element_type=F32)
         + gate(gw2_ref, gb2_ref) * jnp.dot(yc_ref[...], wc_ref[...], preferred_element_type=F32))
    part = jnp.dot(m.astype(BF16), wo_ref[...], preferred_element_type=F32)

    @pl.when(j == 0)
    def _():
        acc[...] = part

    @pl.when(j > 0)
    def _():
        acc[...] += part

    @pl.when(j == pl.num_programs(1) - 1)
    def _():
        o_ref[...] = x_ref[...] + gt_ref[0] * acc[...]


def _merge(h2d, ya, yb, yc, gate_w, gate_b, wa, wb, wc, w_out, x2d, gt, seq):
    t, d = x2d.shape
    tm = min(512, seq)
    tn = 512
    nj = d // tn
    per_batch = seq // tm
    gw = gate_w.astype(BF16)
    gb = gate_b.reshape(1, N_BRANCHES * d).astype(F32)
    row = lambda w: pl.BlockSpec((tm, w), lambda i, j: (i, 0))
    gws = [pl.BlockSpec((d, tn), (lambda i, j, r=r: (0, r * nj + j))) for r in range(N_BRANCHES)]
    gbs = [pl.BlockSpec((1, tn), (lambda i, j, r=r: (0, r * nj + j))) for r in range(N_BRANCHES)]
    return pl.pallas_call(
        _merge_kernel,
        grid=(t // tm, nj),
        in_specs=[row(d), row(ya.shape[1]), row(yb.shape[1]), row(yc.shape[1]),
                  *gws, *gbs,
                  pl.BlockSpec((wa.shape[0], tn), lambda i, j: (0, j)),
                  pl.BlockSpec((wb.shape[0], tn), lambda i, j: (0, j)),
                  pl.BlockSpec((wc.shape[0], tn), lambda i, j: (0, j)),
                  pl.BlockSpec((tn, d), lambda i, j: (j, 0)),
                  row(d),
                  pl.BlockSpec((1, 1, d), lambda i, j: (i // per_batch, 0, 0))],
        out_specs=row(d),
        out_shape=jax.ShapeDtypeStruct((t, d), F32),
        scratch_shapes=[pltpu.VMEM((tm, d), F32)],
        compiler_params=_cparams(("parallel", "arbitrary")),
        name="merge",
    )(h2d, ya, yb, yc, gw, gw, gw, gb, gb, gb,
      wa.astype(BF16), wb.astype(BF16), wc.astype(BF16), w_out.astype(BF16), x2d, gt)


def _ffn_kernel(h_ref, w1_ref, w3_ref, w2_ref, x_ref, gt_ref, o_ref, acc):
    j = pl.program_id(1)
    h = h_ref[...]
    a = jnp.dot(h, w1_ref[...], preferred_element_type=F32)
    b = jnp.dot(h, w3_ref[...], preferred_element_type=F32)
    part = jnp.dot((a * _sigmoid(a) * b).astype(BF16), w2_ref[...], preferred_element_type=F32)

    @pl.when(j == 0)
    def _():
        acc[...] = part

    @pl.when(j > 0)
    def _():
        acc[...] += part

    @pl.when(j == pl.num_programs(1) - 1)
    def _():
        o_ref[...] = x_ref[...] + gt_ref[0] * acc[...]


def _ffn(h2d, w1, w3, w2, x2d, gt, seq):
    t, d = x2d.shape
    f = w1.shape[1]
    tm = min(512, seq)
    tf = 512
    per_batch = seq // tm
    row = pl.BlockSpec((tm, d), lambda i, j: (i, 0))
    return pl.pallas_call(
        _ffn_kernel,
        grid=(t // tm, f // tf),
        in_specs=[row,
                  pl.BlockSpec((d, tf), lambda i, j: (0, j)),
                  pl.BlockSpec((d, tf), lambda i, j: (0, j)),
                  pl.BlockSpec((tf, d), lambda i, j: (j, 0)),
                  row,
                  pl.BlockSpec((1, 1, d), lambda i, j: (i // per_batch, 0, 0))],
        out_specs=row,
        out_shape=jax.ShapeDtypeStruct((t, d), F32),
        scratch_shapes=[pltpu.VMEM((tm, d), F32)],
        compiler_params=_cparams(("parallel", "arbitrary")),
        name="ffn",
    )(h2d, w1.astype(BF16), w3.astype(BF16), w2.astype(BF16), x2d, gt)


def _project_inputs(h2d, w_in, na_qnorm_g, na_knorm_g, da_qnorm_g, da_knorm_g):
    segments = ((2 * NA_W, "rms128"), (NA_W, "bf16"), (2 * DA_QK_W, "rms64"), (DA_V_W, "bf16"),
                (2 * GDN_K_W + 2 * GDN_V_W, "f32"))
    roles = [role for width, role in segments for _ in range(width // PROJ_TN)]
    n_main = sum(width for width, _ in segments)
    gain = jnp.concatenate([jnp.tile(na_qnorm_g * HEAD_DIM ** -0.5, NA_HEADS),
                            jnp.tile(na_knorm_g, NA_HEADS),
                            jnp.ones((NA_W,), F32),
                            jnp.tile(da_qnorm_g * (LOG2E * DA_QK_DIM ** -0.5), 2 * DA_HEADS),
                            jnp.tile(da_knorm_g, 2 * DA_HEADS),
                            jnp.ones((DA_V_W + 2 * GDN_K_W + 2 * GDN_V_W,), F32)])
    attn, gdn_u = _proj_mixers(h2d, w_in[:, :n_main], gain, roles)
    w_lg = jnp.pad(w_in[:, n_main:], ((0, 0), (0, LANES - 4 * GDN_HEADS)))
    return attn, gdn_u, _proj(h2d, w_lg, F32)


def _layer(l, x, mod, norm1_g, norm2_g, w_in, gate_w, gate_b, na_qnorm_g, na_knorm_g, na_rpb,
           da_qnorm_g, da_knorm_g, da_lambda, da_subln_g, gdn_conv_w, gdn_a_log, gdn_dt_bias,
           gdn_norm_g, w_branch_a, w_branch_b, w_branch_c, w_out, ffn_w1, ffn_w3, ffn_w2):
    b, s, d = x.shape
    t = b * s
    sh1, sc1, gt1, sh2, sc2, gt2 = [mod[:, i * d:(i + 1) * d] for i in range(6)]

    h = _norm_mod(x, norm1_g, sc1, sh1)
    h2d = h.reshape(t, d)

    attn, gdn_u, gd_lg = _project_inputs(h2d, w_in, na_qnorm_g, na_knorm_g, da_qnorm_g, da_knorm_g)
    attn = attn.reshape(b, s, -1)

    ya = _natten(attn, na_rpb)
    lam_init = 0.8 - 0.6 * math.exp(-0.3 * l)
    da0 = 3 * NA_W
    yb = _diff_attn(attn[:, :, da0:da0 + 2 * DA_QK_W], attn[:, :, da0 + 2 * DA_QK_W:],
                    da_lambda, da_subln_g, lam_init)
    yc = _gated_deltanet(gdn_u.reshape(b, s, -1), gd_lg.reshape(b, s, -1), gdn_conv_w, gdn_a_log,
                         gdn_dt_bias, gdn_norm_g)

    x2d = _merge(h2d, ya.reshape(t, -1), yb.reshape(t, -1), yc.reshape(t, -1),
                 gate_w, gate_b, w_branch_a, w_branch_b, w_branch_c, w_out,
                 x.reshape(t, d), gt1.reshape(b, 1, d), s)

    h2 = _norm_mod(x2d.reshape(b, s, d), norm2_g, sc2, sh2)
    x2d = _ffn(h2.reshape(t, d), ffn_w1, ffn_w3, ffn_w2, x2d, gt2.reshape(b, 1, d), s)
    return x2d.reshape(b, s, d)


def kernel(x, c, ada_w, ada_b, norm1_g, norm2_g, w_in, gate_w, gate_b, na_qnorm_g, na_knorm_g, na_rpb, da_qnorm_g, da_knorm_g, da_lambda, da_subln_g, gdn_conv_w, gdn_a_log, gdn_dt_bias, gdn_norm_g, w_branch_a, w_branch_b, w_branch_c, w_out, ffn_w1, ffn_w3, ffn_w2):
    depth = ada_w.shape[0]
    mod = _adaln(c, ada_w, ada_b)
    per_layer = (norm1_g, norm2_g, w_in, gate_w, gate_b, na_qnorm_g, na_knorm_g, na_rpb,
                 da_qnorm_g, da_knorm_g, da_lambda, da_subln_g, gdn_conv_w, gdn_a_log,
                 gdn_dt_bias, gdn_norm_g, w_branch_a, w_branch_b, w_branch_c, w_out,
                 ffn_w1, ffn_w3, ffn_w2)
    for l in range(depth):
        x = _layer(l, x, mod[l], *[p[l] for p in per_layer])
    return x
```

```python
import functools
import math
from typing import NamedTuple

import jax
import jax.numpy as jnp
from jax import lax
from jax.experimental import pallas as pl
from jax.experimental.pallas import tpu as pltpu

F32 = jnp.float32
BF16 = jnp.bfloat16

GRID_W = 64
HEAD_DIM = 128
NA_HEADS = 4
NA_WIN_R = 8
NA_WIN_C = 16
DA_HEADS = 4
DA_QK_DIM = 64
DA_V_DIM = 128
GDN_HEADS = 8
GDN_DK = 128
GDN_DV = 128
GDN_CONV = 5
GDN_CHUNK = 64
GDN_SUPER = 256
N_BRANCHES = 3
RMS_EPS = 1e-6

NA_W = NA_HEADS * HEAD_DIM
DA_QK_W = DA_HEADS * 2 * DA_QK_DIM
DA_V_W = DA_HEADS * DA_V_DIM
GDN_K_W = GDN_HEADS * GDN_DK
GDN_V_W = GDN_HEADS * GDN_DV

LANES = 128
VMEM_LIMIT = 56 * 1024 * 1024
NEG_BIG = -1e30

_LOG2_LANES = LANES.bit_length() - 1
_LOG2_CHUNK = GDN_CHUNK.bit_length() - 1

_NT = (((1,), (1,)), ((), ()))
_TN = (((0,), (0,)), ((), ()))


def _cparams(sem):
    return pltpu.CompilerParams(dimension_semantics=sem, vmem_limit_bytes=VMEM_LIMIT)


def _sigmoid(x):
    return 1.0 / (1.0 + jnp.exp(-x))


def _dot(a, b):
    return jnp.dot(a.astype(BF16), b.astype(BF16), preferred_element_type=F32)


def _dot_nt(a, b):
    return lax.dot_general(a.astype(BF16), b.astype(BF16), _NT, preferred_element_type=F32)


def _mod_kernel(c_ref, w_ref, b_ref, o_ref):
    c = c_ref[...]
    a = c * _sigmoid(c)
    o_ref[0] = jnp.dot(a, w_ref[0], preferred_element_type=F32,
                       precision=lax.Precision.HIGHEST) + b_ref[0]


def _adaln(c, ada_w, ada_b):
    depth, d, n = ada_w.shape
    b = c.shape[0]
    rows = 8
    cp = jnp.zeros((rows, d), F32).at[:b].set(c)
    tn = 1024
    out = pl.pallas_call(
        _mod_kernel,
        grid=(depth, n // tn),
        in_specs=[pl.BlockSpec((rows, d), lambda l, j: (0, 0)),
                  pl.BlockSpec((1, d, tn), lambda l, j: (l, 0, j)),
                  pl.BlockSpec((1, 1, tn), lambda l, j: (l, 0, j))],
        out_specs=pl.BlockSpec((1, rows, tn), lambda l, j: (l, 0, j)),
        out_shape=jax.ShapeDtypeStruct((depth, rows, n), F32),
        compiler_params=_cparams(("arbitrary", "arbitrary")),
        name="adaln_mod",
    )(cp, ada_w, ada_b.reshape(depth, 1, n))
    return out[:, :b]


def _norm_kernel(x_ref, g_ref, sc_ref, sh_ref, o_ref):
    x = x_ref[0]
    y = x * lax.rsqrt(jnp.mean(x * x, axis=-1, keepdims=True) + RMS_EPS) * g_ref[...]
    o_ref[0] = (y * (1.0 + sc_ref[0]) + sh_ref[0]).astype(o_ref.dtype)


def _norm_mod(x, g, sc, sh):
    b, s, d = x.shape
    ts = min(512, s)
    return pl.pallas_call(
        _norm_kernel,
        grid=(b, s // ts),
        in_specs=[pl.BlockSpec((1, ts, d), lambda i, j: (i, j, 0)),
                  pl.BlockSpec((1, d), lambda i, j: (0, 0)),
                  pl.BlockSpec((1, 1, d), lambda i, j: (i, 0, 0)),
                  pl.BlockSpec((1, 1, d), lambda i, j: (i, 0, 0))],
        out_specs=pl.BlockSpec((1, ts, d), lambda i, j: (i, j, 0)),
        out_shape=jax.ShapeDtypeStruct((b, s, d), BF16),
        compiler_params=_cparams(("parallel", "parallel")),
        name="norm_mod",
    )(x, g.reshape(1, d), sc.reshape(b, 1, d), sh.reshape(b, 1, d))


def _group_rms_store(acc, gain_ref, o_ref, group):
    tn = acc.shape[1]
    for s in range(tn // LANES):
        sl = slice(s * LANES, (s + 1) * LANES)
        a = acc[:, sl]
        sq = a * a
        if group == LANES:
            ms = jnp.sum(sq, axis=-1, keepdims=True) * (1.0 / LANES)
        else:
            lo = lax.broadcasted_iota(jnp.int32, a.shape, 1) < group
            s_lo = jnp.sum(jnp.where(lo, sq, 0.0), axis=-1, keepdims=True)
            s_hi = jnp.sum(jnp.where(lo, 0.0, sq), axis=-1, keepdims=True)
            ms = jnp.where(lo, s_lo, s_hi) * (1.0 / group)
        y = a * lax.rsqrt(ms + RMS_EPS) * gain_ref[:, sl]
        o_ref[:, sl] = y.astype(o_ref.dtype)


def _proj_kernel(h_ref, w_ref, *rest, group):
    acc = jnp.dot(h_ref[...], w_ref[...], preferred_element_type=F32)
    if group is None:
        (o_ref,) = rest
        o_ref[...] = acc.astype(o_ref.dtype)
    else:
        gain_ref, o_ref = rest
        _group_rms_store(acc, gain_ref, o_ref, group)


def _proj(h2d, w, out_dtype, group=None, gain=None):
    t, d = h2d.shape
    n = w.shape[1]
    tm = min(1024, t)
    tn = min(512, n)
    in_specs = [pl.BlockSpec((tm, d), lambda i, j: (i, 0)),
                pl.BlockSpec((d, tn), lambda i, j: (0, j))]
    args = [h2d, w.astype(BF16)]
    if group is not None:
        assert group in (LANES // 2, LANES)
        in_specs.append(pl.BlockSpec((1, tn), lambda i, j: (0, j)))
        args.append(gain.reshape(1, n).astype(F32))
    return pl.pallas_call(
        functools.partial(_proj_kernel, group=group),
        grid=(t // tm, n // tn),
        in_specs=in_specs,
        out_specs=pl.BlockSpec((tm, tn), lambda i, j: (i, j)),
        out_shape=jax.ShapeDtypeStruct((t, n), out_dtype),
        compiler_params=_cparams(("parallel", "arbitrary")),
        name="proj",
    )(*args)


PROJ_TN = 512


def _proj_mixers_kernel(h_ref, w_ref, gain_ref, attn_ref, gdn_ref, *, roles):
    j = pl.program_id(1)
    acc = jnp.dot(h_ref[...], w_ref[...], preferred_element_type=F32)

    def tiles_with(role):
        cond = None
        for n, r in enumerate(roles):
            if r == role:
                cond = (j == n) if cond is None else (cond | (j == n))
        return cond

    @pl.when(tiles_with("rms128"))
    def _():
        _group_rms_store(acc, gain_ref, attn_ref, LANES)

    @pl.when(tiles_with("rms64"))
    def _():
        _group_rms_store(acc, gain_ref, attn_ref, LANES // 2)

    @pl.when(tiles_with("bf16"))
    def _():
        attn_ref[...] = acc.astype(attn_ref.dtype)

    @pl.when(tiles_with("f32"))
    def _():
        gdn_ref[...] = acc


def _proj_mixers(h2d, w, gain, roles):
    t, d = h2d.shape
    n = w.shape[1]
    tm = min(1024, t)
    tn = PROJ_TN
    n_attn = sum(r != "f32" for r in roles)
    assert len(roles) * tn == n and all(r == "f32" for r in roles[n_attn:])
    return pl.pallas_call(
        functools.partial(_proj_mixers_kernel, roles=tuple(roles)),
        grid=(t // tm, len(roles)),
        in_specs=[pl.BlockSpec((tm, d), lambda i, j: (i, 0)),
                  pl.BlockSpec((d, tn), lambda i, j: (0, j)),
                  pl.BlockSpec((1, tn), lambda i, j: (0, j))],
        out_specs=[pl.BlockSpec((tm, tn), lambda i, j: (i, jnp.minimum(j, n_attn - 1))),
                   pl.BlockSpec((tm, tn), lambda i, j: (i, jnp.maximum(j - n_attn, 0)))],
        out_shape=[jax.ShapeDtypeStruct((t, n_attn * tn), BF16),
                   jax.ShapeDtypeStruct((t, n - n_attn * tn), F32)],
        compiler_params=_cparams(("parallel", "arbitrary")),
        name="proj_mixers",
    )(h2d, w.astype(BF16), gain.reshape(1, n).astype(F32))


NA_ROWS_PER_STEP = 8
NA_BAND = NA_WIN_R * GRID_W


def _na_bias_table(rpb):
    h = rpb.shape[0]
    w = jnp.arange(GRID_W)
    col_start = jnp.clip(w - NA_WIN_C // 2, 0, GRID_W - NA_WIN_C)
    ck = jnp.arange(GRID_W)
    inwin = (ck[None, :] >= col_start[:, None]) & (ck[None, :] < col_start[:, None] + NA_WIN_C)
    cbi = jnp.clip(ck[None, :] - w[:, None] + (NA_WIN_C - 1), 0, 2 * NA_WIN_C - 2)
    e = jnp.arange(NA_WIN_R)
    j = jnp.arange(NA_WIN_R)
    rbi = j[None, :] + (NA_WIN_R - 1) - e[:, None]
    tab = rpb[:, rbi]
    tab = tab[:, :, :, cbi]
    tab = jnp.where(inwin[None, None, None], tab, NEG_BIG)
    return tab.transpose(1, 0, 3, 2, 4).reshape(NA_WIN_R, h, GRID_W, NA_BAND).astype(F32)


def _na_kernel(q_ref, kp_ref, kc_ref, kn_ref, vp_ref, vc_ref, vn_ref, tab_ref, o_ref,
               kbuf, vbuf, *, rows):
    rb = pl.program_id(1)
    blk = NA_ROWS_PER_STEP * GRID_W
    for n, (kr, vr) in enumerate(((kp_ref, vp_ref), (kc_ref, vc_ref), (kn_ref, vn_ref))):
        kbuf[n * blk:(n + 1) * blk, :] = kr[0]
        vbuf[n * blk:(n + 1) * blk, :] = vr[0]
    heads = [slice(h * HEAD_DIM, (h + 1) * HEAD_DIM) for h in range(NA_HEADS)]

    def band(i):
        r = rb * NA_ROWS_PER_STEP + i
        r_start = jnp.clip(r - NA_WIN_R // 2, 0, rows - NA_WIN_R)
        loc = pl.multiple_of((r_start - (rb - 1) * NA_ROWS_PER_STEP) * GRID_W, GRID_W)
        return r - r_start, loc, slice(i * GRID_W, (i + 1) * GRID_W)

    def scores(i):
        e, loc, rs = band(i)
        return [lax.dot_general(q_ref[0, rs, hs], kbuf[pl.ds(loc, NA_BAND), hs], _NT,
                                preferred_element_type=F32) + tab_ref[e, h]
                for h, hs in enumerate(heads)]

    def finish(i, s_heads):
        _, loc, rs = band(i)
        for hs, s in zip(heads, s_heads):
            p = jnp.exp(s - jnp.max(s, axis=-1, keepdims=True))
            l = jnp.sum(p, axis=-1, keepdims=True)
            o = jnp.dot(p.astype(BF16), vbuf[pl.ds(loc, NA_BAND), hs], preferred_element_type=F32) / l
            o_ref[0, rs, hs] = o.astype(o_ref.dtype)

    pending = scores(0)
    for i in range(1, NA_ROWS_PER_STEP):
        nxt = scores(i)
        finish(i - 1, pending)
        pending = nxt
    finish(NA_ROWS_PER_STEP - 1, pending)


def _natten(qkv, rpb):
    b, s, _ = qkv.shape
    rows = s // GRID_W
    blk = NA_ROWS_PER_STEP * GRID_W
    nrb = s // blk
    tab = _na_bias_table(rpb)

    def at(off, col):
        return lambda i, j: (i, jnp.clip(j + off, 0, nrb - 1), col)

    return pl.pallas_call(
        functools.partial(_na_kernel, rows=rows),
        grid=(b, nrb),
        in_specs=[pl.BlockSpec((1, blk, NA_W), at(0, 0)),
                  pl.BlockSpec((1, blk, NA_W), at(-1, 1)),
                  pl.BlockSpec((1, blk, NA_W), at(0, 1)),
                  pl.BlockSpec((1, blk, NA_W), at(1, 1)),
                  pl.BlockSpec((1, blk, NA_W), at(-1, 2)),
                  pl.BlockSpec((1, blk, NA_W), at(0, 2)),
                  pl.BlockSpec((1, blk, NA_W), at(1, 2)),
                  pl.BlockSpec(tab.shape, lambda i, j: (0, 0, 0, 0))],
        out_specs=pl.BlockSpec((1, blk, NA_W), at(0, 0)),
        out_shape=jax.ShapeDtypeStruct((b, s, NA_W), BF16),
        scratch_shapes=[pltpu.VMEM((3 * blk, NA_W), BF16), pltpu.VMEM((3 * blk, NA_W), BF16)],
        compiler_params=_cparams(("parallel", "parallel")),
        name="natten",
    )(*([qkv] * 7), tab)


DA_TILE = 512
DA_BLOCK = 256
DA_HEADS_PER_STEP = 2
DA_ONES_ROWS = 16
LOG2E = math.log2(math.e)


def _da_slopes():
    return [LOG2E * 2.0 ** (-8.0 * (i + 1) / DA_HEADS) for i in range(DA_HEADS)]


def _bf16_parts(x):
    p1 = x.astype(BF16)
    r1 = x - p1.astype(F32)
    p2 = r1.astype(BF16)
    p3 = (r1 - p2.astype(F32)).astype(BF16)
    return [p1, p2, p3]


def _da_pos_tables(tile):
    slopes = jnp.asarray(_da_slopes(), F32)[:, None]
    pos = jnp.arange(tile)
    parts = _bf16_parts(slopes * pos.astype(F32)[None, :])
    ones = [jnp.ones((DA_HEADS, tile), BF16)] * 3
    zpad = jnp.zeros((DA_HEADS, DA_QK_DIM - 6, tile), BF16)
    kfeat = jnp.concatenate([jnp.stack(parts + ones, axis=1), zpad], axis=1)
    qfeat = jnp.concatenate([jnp.stack(ones + [-p for p in parts], axis=1), zpad], axis=1)
    dist = jnp.abs(pos[:, None] - pos[None, :]).astype(F32)
    bdiag = -slopes[:, :, None] * dist[None]
    return qfeat, kfeat, bdiag


def _da_kernel(qt_ref, ka_ref, vt_ref, qpos_ref, ones_ref, bdiag_ref, slope_ref, lam_ref, g_ref,
               o_ref, qa_sc, st0_sc, st1_sc, mt0_sc, mt1_sc, m_sc, acc_sc, *, tile, lam_init):
    hg = pl.program_id(1)
    qi = pl.program_id(2)
    nk = vt_ref.shape[2]
    nv = nk - 1
    half = DA_QK_DIM
    nprob = 2 * DA_HEADS_PER_STEP
    ones_rows = ones_ref[...]
    slopes = [slope_ref[hg * DA_HEADS_PER_STEP + hh] for hh in range(DA_HEADS_PER_STEP)]

    for hh in range(DA_HEADS_PER_STEP):
        qt = qt_ref[0, hh * LANES:(hh + 1) * LANES]
        qpos = qpos_ref[hh]
        for sel, pp in enumerate((qpos, -qpos, jnp.zeros_like(qpos))):
            qa_sc[sel, 2 * hh] = jnp.concatenate([qt[0:half], pp], axis=0)
            qa_sc[sel, 2 * hh + 1] = jnp.concatenate([pp, qt[half:2 * half]], axis=0)

    def keys(p, start, size):
        return ka_ref[0, p // 2, p % 2, pl.ds(start, size), :]

    def values(p, t):
        return jnp.concatenate([vt_ref[0, p // 2, t], ones_rows], axis=0)

    start = pl.multiple_of(qi * tile, tile)
    diag = [jnp.dot(keys(p, start, tile), qa_sc[2, p], preferred_element_type=F32) + bdiag_ref[p // 2]
            for p in range(nprob)]
    for p, st in enumerate(diag):
        m_new = jnp.max(st, axis=0, keepdims=True)
        pt = jnp.exp2(st - m_new).astype(BF16)
        acc_sc[p] = jnp.dot(values(p, qi), pt, preferred_element_type=F32)
        m_sc[p] = m_new

    if nv == 0:
        pass
    else:
        blocks = [slice(i, i + DA_BLOCK) for i in range(0, tile, DA_BLOCK)]

        def key_tile(v):
            after = (v >= qi).astype(jnp.int32)
            return v + after, after

        def score_units(v, st_sc, mt_sc):
            t, after = key_tile(v)
            for p in range(nprob):
                for qs in blocks:
                    mcol = None
                    for ks in blocks:
                        start = pl.multiple_of(t * tile + ks.start, DA_BLOCK)
                        st = jnp.dot(keys(p, start, DA_BLOCK), qa_sc[after, p, :, qs],
                                     preferred_element_type=F32)
                        st_sc[p, ks, qs] = st
                        mk = jnp.max(st, axis=0, keepdims=True)
                        mcol = mk if mcol is None else jnp.maximum(mcol, mk)
                        if ks is blocks[-1]:
                            mt_sc[p, :, qs] = mcol
                        yield

        def softmax_units(v, st_sc, mt_sc):
            t, _ = key_tile(v)
            dist = (tile * jnp.abs(t - qi)).astype(F32)
            stats = []
            for p in range(nprob):
                shift = -slopes[p // 2] * dist
                m_old = m_sc[p]
                m_new = jnp.maximum(m_old, mt_sc[p] + shift)
                stats.append((jnp.exp2(m_old - m_new), m_new - shift))
                m_sc[p] = m_new
            for p, (alpha, msub) in enumerate(stats):
                vat = values(p, t)
                for qs in blocks:
                    acc = alpha[:, qs] * acc_sc[p, :, qs]
                    for ks in blocks:
                        pt = jnp.exp2(st_sc[p, ks, qs] - msub[:, qs]).astype(BF16)
                        acc = acc + jnp.dot(vat[:, ks], pt, preferred_element_type=F32)
                        if ks is blocks[-1]:
                            acc_sc[p, :, qs] = acc
                        yield

        def overlapped(scores, softmax):
            if scores is not None:
                next(scores)
            for _ in softmax:
                if scores is not None:
                    next(scores, None)

        bufs = ((st0_sc, mt0_sc), (st1_sc, mt1_sc))
        overlapped(None, score_units(0, *bufs[0]))

        def body(v, carry):
            for parity in range(2):
                @pl.when((v & 1) == parity)
                def _():
                    overlapped(score_units(v + 1, *bufs[1 - parity]), softmax_units(v, *bufs[parity]))
            return carry

        lax.fori_loop(0, nv - 1, body, 0)
        overlapped(None, softmax_units(nv - 1, *bufs[(nv - 1) % 2]))

    lp = lam_ref[...]
    lam = (jnp.exp(jnp.sum(lp[0:1] * lp[1:2], axis=-1, keepdims=True))
           - jnp.exp(jnp.sum(lp[2:3] * lp[3:4], axis=-1, keepdims=True)) + lam_init)
    for hh in range(DA_HEADS_PER_STEP):
        a0 = acc_sc[2 * hh]
        a1 = acc_sc[2 * hh + 1]
        o0 = a0[0:DA_V_DIM] / a0[DA_V_DIM:DA_V_DIM + 1]
        o1 = a1[0:DA_V_DIM] / a1[DA_V_DIM:DA_V_DIM + 1]
        a = o0 - lam * o1
        y = a * lax.rsqrt(jnp.mean(a * a, axis=0, keepdims=True) + RMS_EPS) * g_ref[...]
        o_ref[0, :, hh * DA_V_DIM:(hh + 1) * DA_V_DIM] = (y * (1.0 - lam_init)).T.astype(o_ref.dtype)


def _diff_attn(qk, v, lam_p, subln_g, lam_init):
    b, s, _ = v.shape
    tile = min(DA_TILE, s)
    nt = s // tile
    nh = DA_HEADS
    qfeat, kfeat, bdiag = _da_pos_tables(tile)
    slopes = jnp.asarray(_da_slopes(), F32)

    qt = qk[:, :, :DA_QK_W].transpose(0, 2, 1)
    k4 = qk[:, :, DA_QK_W:].reshape(b, s, nh, 2, DA_QK_DIM)
    kp = jnp.broadcast_to(jnp.tile(kfeat.transpose(0, 2, 1), (1, nt, 1))[None], (b, nh, s, DA_QK_DIM))
    k0 = jnp.concatenate([k4[:, :, :, 0].transpose(0, 2, 1, 3), kp], axis=-1)
    k1 = jnp.concatenate([kp, k4[:, :, :, 1].transpose(0, 2, 1, 3)], axis=-1)
    ka = jnp.stack([k0, k1], axis=2)
    vt = v.reshape(b, nt, tile, nh, DA_V_DIM).transpose(0, 3, 1, 4, 2)
    ones_rows = jnp.zeros((DA_ONES_ROWS, tile), BF16).at[0].set(1.0)

    hps = DA_HEADS_PER_STEP
    nprob = 2 * hps
    resident = pl.Buffered(1)
    return pl.pallas_call(
        functools.partial(_da_kernel, tile=tile, lam_init=lam_init),
        grid=(b, nh // hps, nt),
        in_specs=[pl.BlockSpec((1, hps * LANES, tile), lambda i, h, q: (i, h, q)),
                  pl.BlockSpec((1, hps, 2, s, LANES), lambda i, h, q: (i, h, 0, 0, 0),
                               pipeline_mode=resident),
                  pl.BlockSpec((1, hps, nt, DA_V_DIM, tile), lambda i, h, q: (i, h, 0, 0, 0),
                               pipeline_mode=resident),
                  pl.BlockSpec((hps, DA_QK_DIM, tile), lambda i, h, q: (h, 0, 0)),
                  pl.BlockSpec((DA_ONES_ROWS, tile), lambda i, h, q: (0, 0)),
                  pl.BlockSpec((hps, tile, tile), lambda i, h, q: (h, 0, 0)),
                  pl.BlockSpec(memory_space=pltpu.SMEM),
                  pl.BlockSpec((4, DA_QK_DIM), lambda i, h, q: (0, 0)),
                  pl.BlockSpec((DA_V_DIM, 1), lambda i, h, q: (0, 0))],
        out_specs=pl.BlockSpec((1, tile, hps * DA_V_DIM), lambda i, h, q: (i, q, h)),
        out_shape=jax.ShapeDtypeStruct((b, s, DA_V_W), BF16),
        scratch_shapes=[pltpu.VMEM((3, nprob, LANES, tile), BF16),
                        pltpu.VMEM((nprob, tile, tile), F32),
                        pltpu.VMEM((nprob, tile, tile), F32),
                        pltpu.VMEM((nprob, 1, tile), F32),
                        pltpu.VMEM((nprob, 1, tile), F32),
                        pltpu.VMEM((nprob, 1, tile), F32),
                        pltpu.VMEM((nprob, DA_V_DIM + DA_ONES_ROWS, tile), F32)],
        compiler_params=_cparams(("parallel", "parallel", "arbitrary")),
        name="diff_attn",
    )(qt, ka, vt, qfeat, ones_rows, bdiag, slopes, lam_p.astype(F32),
      subln_g.reshape(DA_V_DIM, 1).astype(F32))


def _gdn_gate_kernel(lg_ref, alog_ref, dt_ref, beta_ref, gc_ref, gt_ref):
    lg = lg_ref[0]
    nh = 2 * GDN_HEADS
    beta_ref[0] = _sigmoid(lg[0:nh])
    z = lg[nh:2 * nh] + dt_ref[...]
    sp = jnp.maximum(z, 0.0) + jnp.log(1.0 + jnp.exp(-jnp.abs(z)))
    g = -jnp.exp(alog_ref[...]) * sp
    ts = g.shape[1]
    pos = lax.broadcasted_iota(jnp.int32, g.shape, 1) & (GDN_CHUNK - 1)
    pre = g
    suf = g
    sh = 1
    while sh < GDN_CHUNK:
        pre = pre + jnp.where(pos >= sh, pltpu.roll(pre, sh, axis=1), 0.0)
        suf = suf + jnp.where(pos < GDN_CHUNK - sh, pltpu.roll(suf, ts - sh, axis=1), 0.0)
        sh *= 2
    bwd = lax.broadcasted_iota(jnp.int32, g.shape, 0) >= GDN_HEADS
    gc_ref[0] = jnp.where(bwd, suf, pre)
    gt_ref[0] = jnp.where(bwd, pre, suf) - g


def _gdn_gates(lgt, a_log, dt_bias):
    b, _, s = lgt.shape
    nh = 2 * GDN_HEADS
    ts = min(2048, s)
    shp = jax.ShapeDtypeStruct((b, nh, s), F32)
    spec = pl.BlockSpec((1, nh, ts), lambda i, j: (i, 0, j))
    return pl.pallas_call(
        _gdn_gate_kernel,
        grid=(b, s // ts),
        in_specs=[pl.BlockSpec((1, 2 * nh, ts), lambda i, j: (i, 0, j)),
                  pl.BlockSpec((nh, 1), lambda i, j: (0, 0)),
                  pl.BlockSpec((nh, 1), lambda i, j: (0, 0))],
        out_specs=[spec, spec, spec],
        out_shape=[shp, shp, shp],
        compiler_params=_cparams(("parallel", "parallel")),
        name="gdn_gates",
    )(lgt, a_log.reshape(nh, 1).astype(F32), dt_bias.reshape(nh, 1).astype(F32))


GDN_HALO = 8


def _gdn_conv_kernel(x_ref, p_ref, n_ref, w_ref, o_ref, ext, *, ts):
    cg = pl.program_id(0)
    i = pl.program_id(2)
    last = pl.num_programs(2) - 1
    ext[0:GDN_HALO, :] = jnp.where(i > 0, p_ref[0], 0.0)
    ext[GDN_HALO:GDN_HALO + ts, :] = x_ref[0]
    ext[GDN_HALO + ts:2 * GDN_HALO + ts, :] = jnp.where(i < last, n_ref[0], 0.0)
    y = None
    for tap in range(GDN_CONV):
        off = GDN_HALO + tap - GDN_CONV // 2
        term = ext[off:off + ts, :] * w_ref[0, tap:tap + 1, :]
        y = term if y is None else y + term
    y = y * _sigmoid(y)
    qscale = jnp.where(cg == 0, GDN_DK ** -0.5, 1.0).astype(F32)
    is_v = cg == 2
    for h in range(GDN_HEADS):
        hs = slice(h * GDN_DK, (h + 1) * GDN_DK)
        a = y[:, hs]
        nrm = lax.rsqrt(jnp.sum(a * a, axis=-1, keepdims=True) + RMS_EPS) * qscale
        o_ref[0, 0, :, hs] = a * jnp.where(is_v, 1.0, nrm)


def _gdn_conv(u, conv_w):
    b, s, _ = u.shape
    cw = GDN_K_W
    ts = min(512, s)
    nb = s // GDN_HALO
    per = ts // GDN_HALO
    w3 = conv_w.reshape(GDN_CONV, 3, cw).transpose(1, 0, 2)
    w3 = jnp.pad(w3, ((0, 0), (0, 8 - GDN_CONV), (0, 0))).astype(F32)
    return pl.pallas_call(
        functools.partial(_gdn_conv_kernel, ts=ts),
        grid=(3, b, s // ts),
        in_specs=[pl.BlockSpec((1, ts, cw), lambda c, i, j: (i, j, c)),
                  pl.BlockSpec((1, GDN_HALO, cw),
                               lambda c, i, j: (i, jnp.maximum(j * per - 1, 0), c)),
                  pl.BlockSpec((1, GDN_HALO, cw),
                               lambda c, i, j: (i, jnp.minimum((j + 1) * per, nb - 1), c)),
                  pl.BlockSpec((1, 8, cw), lambda c, i, j: (c, 0, 0))],
        out_specs=pl.BlockSpec((1, 1, ts, cw), lambda c, i, j: (c, i, j, 0)),
        out_shape=jax.ShapeDtypeStruct((3, b, s, cw), F32),
        scratch_shapes=[pltpu.VMEM((ts + 2 * GDN_HALO, cw), F32)],
        compiler_params=_cparams(("parallel", "parallel", "parallel")),
        name="gdn_conv",
    )(u, u, u, w3)


class _GdnChain(NamedTuple):
    q: jax.Array
    k: jax.Array
    v: jax.Array
    cols: jax.Array
    gr: jax.Array
    rev: bool
    state: object
    out: object


def _gdn_chains(chains):
    sc = GDN_SUPER
    ck = GDN_CHUNK
    nchunk = sc // ck
    row = lax.broadcasted_iota(jnp.int32, (ck, sc), 0)
    lane = lax.broadcasted_iota(jnp.int32, (ck, sc), 1)
    col = lane & (ck - 1)
    lane_chunk = lane >> _LOG2_CHUNK
    eye = jnp.where(row == col, 1.0, 0.0).astype(F32)
    masks = {False: (row >= col, row > col), True: (row <= col, row < col)}
    ii = lax.broadcasted_iota(jnp.int32, (sc, sc), 0)
    jj = lax.broadcasted_iota(jnp.int32, (sc, sc), 1)
    bd_mask = jnp.where((ii >> _LOG2_CHUNK) == (jj >> _LOG2_CHUNK), 1.0, 0.0).astype(BF16)

    def pack(full):
        out = full[0:ck]
        for b in range(1, nchunk):
            out = jnp.where(lane_chunk == b, full[b * ck:(b + 1) * ck], out)
        return out

    def blockdiag(packed_bf16):
        return jnp.concatenate([packed_bf16] * nchunk, axis=0) * bd_mask

    def lanes(c):
        return jnp.broadcast_to(c, (sc, LANES))

    bcast = [(lanes(ch.cols[:, 0:1]), lanes(ch.cols[:, 1:2]), lanes(ch.cols[:, 2:3])) for ch in chains]

    decays, kbs, a_mats = [], [], []
    for ch, (beta, gc, _) in zip(chains, bcast):
        incl, strict = masks[ch.rev]
        gc2 = jnp.concatenate([gc, gc], axis=1)
        decay = jnp.exp(jnp.where(incl, pack(gc2) - ch.gr, NEG_BIG))
        kb = ch.k * beta
        decays.append(decay)
        kbs.append(kb)
        a_mats.append(jnp.where(strict, pack(_dot_nt(kb, ch.k)) * decay, 0.0))

    xs = [-a for a in a_mats]
    ts = [eye + x for x in xs]
    xs = [_dot(x, blockdiag(x.astype(BF16))) for x in xs]
    span = 2
    while span < ck:
        last = 2 * span >= ck
        new_ts, new_xs = [], []
        for t, x in zip(ts, xs):
            w_bd = blockdiag(x.astype(BF16))
            if last:
                new_ts.append(t + _dot(t, w_bd))
                new_xs.append(None)
            else:
                both = _dot(jnp.concatenate([t, x], axis=0), w_bd)
                new_ts.append(t + both[0:ck])
                new_xs.append(both[ck:2 * ck])
        ts, xs = new_ts, new_xs
        span *= 2
    resids = []
    for a, t in zip(a_mats, ts):
        a_hi = a.astype(BF16)
        a_lo = (a - a_hi.astype(F32)).astype(BF16)
        t_hi = t.astype(BF16)
        t_lo = (t - t_hi.astype(F32)).astype(BF16)
        hi = jnp.dot(jnp.concatenate([a_hi, a_lo], axis=0), blockdiag(t_hi), preferred_element_type=F32)
        a_t = hi[0:ck] + hi[ck:2 * ck] + jnp.dot(a_hi, blockdiag(t_lo), preferred_element_type=F32)
        resids.append(eye - t - a_t)
    ts = [t + _dot(t, blockdiag(r.astype(BF16))) for t, r in zip(ts, resids)]

    pre = []
    for ch, (beta, gc, gt), decay, kb, t in zip(chains, bcast, decays, kbs, ts):
        egc = jnp.exp(gc)
        uw = _dot(blockdiag(t.astype(BF16)), jnp.concatenate([ch.v * beta, kb * egc], axis=1))
        pre.append(dict(u=uw[:, 0:GDN_DV], w=uw[:, GDN_DV:], qg=ch.q * egc,
                        ai=pack(_dot_nt(ch.q, ch.k)) * decay, kt_t=(ch.k * jnp.exp(gt)).T,
                        gl=jnp.exp(gc + gt)))

    states = [ch.state[...] for ch in chains]
    outs = [[None] * nchunk for _ in chains]
    for step in range(nchunk):
        order = [nchunk - 1 - step if ch.rev else step for ch in chains]
        spans = [slice(j * ck, (j + 1) * ck) for j in order]
        from_state = [_dot(jnp.concatenate([p["w"][sl], p["qg"][sl]], axis=0), s_mat)
                      for p, sl, s_mat in zip(pre, spans, states)]
        v_news = [p["u"][sl] - r[0:ck] for p, sl, r in zip(pre, spans, from_state)]
        from_v = [_dot(jnp.concatenate([p["ai"][:, sl], p["kt_t"][:, sl]], axis=0), v_new)
                  for p, sl, v_new in zip(pre, spans, v_news)]
        for n, (p, j, r1, r2) in enumerate(zip(pre, order, from_state, from_v)):
            outs[n][j] = r1[ck:2 * ck] + r2[0:ck]
            states[n] = states[n] * p["gl"][j * ck:j * ck + 1, :] + r2[ck:]
    for ch, s_mat, o in zip(chains, states, outs):
        ch.state[...] = s_mat
        ch.out[...] = jnp.concatenate(o, axis=0)


GDN_HEADS_PER_STEP = 8


def _gdn_scan_kernel(qf_ref, kf_ref, vf_ref, cf_ref, rf_ref, qb_ref, kb_ref, vb_ref, cb_ref, rb_ref,
                     of_ref, ob_ref, state):
    hg = pl.program_id(1)

    @pl.when(pl.program_id(2) == 0)
    def _():
        state[...] = jnp.zeros(state.shape, F32)

    per_group = 3 * GDN_HEADS_PER_STEP
    shift = (LANES - per_group * hg) & (LANES - 1)
    cols_fb = (pltpu.roll(cf_ref[0, 0], shift, axis=1), pltpu.roll(cb_ref[0, 0], shift, axis=1))

    chains = []
    for i in range(GDN_HEADS_PER_STEP):
        hd = hg * GDN_HEADS_PER_STEP + i
        ks = slice(i * GDN_DK, (i + 1) * GDN_DK)
        vs = slice(i * GDN_DV, (i + 1) * GDN_DV)
        for rev, (q_ref, k_ref, v_ref, r_ref, o_ref) in enumerate(
                ((qf_ref, kf_ref, vf_ref, rf_ref, of_ref), (qb_ref, kb_ref, vb_ref, rb_ref, ob_ref))):
            chains.append(_GdnChain(
                q=q_ref[0, 0, :, ks], k=k_ref[0, 0, :, ks], v=v_ref[0, 0, :, vs],
                cols=cols_fb[rev][:, 3 * i:3 * i + 3],
                gr=r_ref[0, 0, pl.ds(hd, 1), :], rev=bool(rev),
                state=state.at[2 * i + rev], out=o_ref.at[0, :, vs]))
    _gdn_chains(chains)


def _gdn_scan(qkv, cols, rows):
    _, b, s, _ = qkv.shape
    sc = GDN_SUPER
    nsc = s // sc
    hps = GDN_HEADS_PER_STEP

    def specs(d):
        seq = (lambda n: nsc - 1 - n) if d else (lambda n: n)
        return [pl.BlockSpec((1, 1, sc, hps * GDN_DK), lambda i, h, n: (0, i, seq(n), h)),
                pl.BlockSpec((1, 1, sc, hps * GDN_DK), lambda i, h, n: (1, i, seq(n), h)),
                pl.BlockSpec((1, 1, sc, hps * GDN_DV), lambda i, h, n: (2, i, seq(n), h)),
                pl.BlockSpec((1, 1, sc, LANES), lambda i, h, n: (i, d, seq(n), 0)),
                pl.BlockSpec((1, 1, GDN_HEADS, sc), lambda i, h, n: (i, d, 0, seq(n)))]

    out_shape = jax.ShapeDtypeStruct((b, s, GDN_V_W), F32)
    return pl.pallas_call(
        _gdn_scan_kernel,
        grid=(b, GDN_HEADS // hps, nsc),
        in_specs=specs(0) + specs(1),
        out_specs=[pl.BlockSpec((1, sc, hps * GDN_DV), lambda i, h, n: (i, n, h)),
                   pl.BlockSpec((1, sc, hps * GDN_DV), lambda i, h, n: (i, nsc - 1 - n, h))],
        out_shape=[out_shape, out_shape],
        scratch_shapes=[pltpu.VMEM((2 * hps, GDN_DK, GDN_DV), F32)],
        compiler_params=_cparams(("parallel", "parallel", "arbitrary")),
        name="gdn_scan",
    )(qkv, qkv, qkv, cols, rows, qkv, qkv, qkv, cols, rows)


def _gdn_out_kernel(of_ref, ob_ref, z_ref, g_ref, o_ref):
    o = of_ref[0] + ob_ref[0]
    z = z_ref[0].astype(F32)
    gate = z * _sigmoid(z)
    for h in range(GDN_HEADS):
        hs = slice(h * GDN_DV, (h + 1) * GDN_DV)
        a = o[:, hs]
        y = a * lax.rsqrt(jnp.mean(a * a, axis=-1, keepdims=True) + RMS_EPS) * g_ref[...]
        o_ref[0, :, hs] = (y * gate[:, hs]).astype(o_ref.dtype)


def _gdn_out(o_f, o_b, u, norm_g):
    b, s, w = o_f.shape
    ts = min(512, s)
    spec = pl.BlockSpec((1, ts, w), lambda i, j: (i, j, 0))
    z_spec = pl.BlockSpec((1, ts, w), lambda i, j: (i, j, 3))
    return pl.pallas_call(
        _gdn_out_kernel,
        grid=(b, s // ts),
        in_specs=[spec, spec, z_spec, pl.BlockSpec((1, GDN_DV), lambda i, j: (0, 0))],
        out_specs=spec,
        out_shape=jax.ShapeDtypeStruct((b, s, w), BF16),
        compiler_params=_cparams(("parallel", "parallel")),
        name="gdn_out",
    )(o_f, o_b, u, norm_g.reshape(1, GDN_DV).astype(F32))


def _gated_deltanet(u, lg, conv_w, a_log, dt_bias, norm_g):
    b, s, _ = u.shape
    nh = GDN_HEADS
    lgt = lg[:, :, :4 * nh].transpose(0, 2, 1)
    beta, gc, gt = _gdn_gates(lgt, a_log, dt_bias)
    rows = gc.reshape(b, 2, nh, s)
    cols = jnp.stack([beta, gc, gt], axis=-1).reshape(b, 2, nh, s, 3)
    cols = cols.transpose(0, 1, 3, 2, 4).reshape(b, 2, s, 3 * nh)
    cols = jnp.pad(cols, ((0, 0), (0, 0), (0, 0), (0, LANES - 3 * nh)))
    qkv = _gdn_conv(u, conv_w)
    o_f, o_b = _gdn_scan(qkv, cols, rows)
    return _gdn_out(o_f, o_b, u, norm_g)


def _merge_kernel(h_ref, ya_ref, yb_ref, yc_ref, gw0_ref, gw1_ref, gw2_ref,
                  gb0_ref, gb1_ref, gb2_ref, wa_ref, wb_ref, wc_ref, wo_ref,
                  x_ref, gt_ref, o_ref, acc):
    j = pl.program_id(1)
    h = h_ref[...]

    def gate(gw_ref, gb_ref):
        return _sigmoid(jnp.dot(h, gw_ref[...], preferred_element_type=F32) + gb_ref[...])

    m = (gate(gw0_ref, gb0_ref) * jnp.dot(ya_ref[...], wa_ref[...], preferred_element_type=F32)
         + gate(gw1_ref, gb1_ref) * jnp.dot(yb_ref[...], wb_ref[...], preferred_element_type=F32)
         + gate(gw2_ref, gb2_ref) * jnp.dot(yc_ref[...], wc_ref[...], preferred_element_type=F32))
    part = jnp.dot(m.astype(BF16), wo_ref[...], preferred_element_type=F32)

    @pl.when(j == 0)
    def _():
        acc[...] = part

    @pl.when(j > 0)
    def _():
        acc[...] += part

    @pl.when(j == pl.num_programs(1) - 1)
    def _():
        o_ref[...] = x_ref[...] + gt_ref[0] * acc[...]


def _merge(h2d, ya, yb, yc, gate_w, gate_b, wa, wb, wc, w_out, x2d, gt, seq):
    t, d = x2d.shape
    tm = min(512, seq)
    tn = 512
    nj = d // tn
    per_batch = seq // tm
    gw = gate_w.astype(BF16)
    gb = gate_b.reshape(1, N_BRANCHES * d).astype(F32)
    row = lambda w: pl.BlockSpec((tm, w), lambda i, j: (i, 0))
    gws = [pl.BlockSpec((d, tn), (lambda i, j, r=r: (0, r * nj + j))) for r in range(N_BRANCHES)]
    gbs = [pl.BlockSpec((1, tn), (lambda i, j, r=r: (0, r * nj + j))) for r in range(N_BRANCHES)]
    return pl.pallas_call(
        _merge_kernel,
        grid=(t // tm, nj),
        in_specs=[row(d), row(ya.shape[1]), row(yb.shape[1]), row(yc.shape[1]),
                  *gws, *gbs,
                  pl.BlockSpec((wa.shape[0], tn), lambda i, j: (0, j)),
                  pl.BlockSpec((wb.shape[0], tn), lambda i, j: (0, j)),
                  pl.BlockSpec((wc.shape[0], tn), lambda i, j: (0, j)),
                  pl.BlockSpec((tn, d), lambda i, j: (j, 0)),
                  row(d),
                  pl.BlockSpec((1, 1, d), lambda i, j: (i // per_batch, 0, 0))],
        out_specs=row(d),
        out_shape=jax.ShapeDtypeStruct((t, d), F32),
        scratch_shapes=[pltpu.VMEM((tm, d), F32)],
        compiler_params=_cparams(("parallel", "arbitrary")),
        name="merge",
    )(h2d, ya, yb, yc, gw, gw, gw, gb, gb, gb,
      wa.astype(BF16), wb.astype(BF16), wc.astype(BF16), w_out.astype(BF16), x2d, gt)


def _ffn_kernel(h_ref, w1_ref, w3_ref, w2_ref, x_ref, gt_ref, o_ref, acc):
    j = pl.program_id(1)
    h = h_ref[...]
    a = jnp.dot(h, w1_ref[...], preferred_element_type=F32)
    b = jnp.dot(h, w3_ref[...], preferred_element_type=F32)
    part = jnp.dot((a * _sigmoid(a) * b).astype(BF16), w2_ref[...], preferred_element_type=F32)

    @pl.when(j == 0)
    def _():
        acc[...] = part

    @pl.when(j > 0)
    def _():
        acc[...] += part

    @pl.when(j == pl.num_programs(1) - 1)
    def _():
        o_ref[...] = x_ref[...] + gt_ref[0] * acc[...]


def _ffn(h2d, w1, w3, w2, x2d, gt, seq):
    t, d = x2d.shape
    f = w1.shape[1]
    tm = min(512, seq)
    tf = 512
    per_batch = seq // tm
    row = pl.BlockSpec((tm, d), lambda i, j: (i, 0))
    return pl.pallas_call(
        _ffn_kernel,
        grid=(t // tm, f // tf),
        in_specs=[row,
                  pl.BlockSpec((d, tf), lambda i, j: (0, j)),
                  pl.BlockSpec((d, tf), lambda i, j: (0, j)),
                  pl.BlockSpec((tf, d), lambda i, j: (j, 0)),
                  row,
                  pl.BlockSpec((1, 1, d), lambda i, j: (i // per_batch, 0, 0))],
        out_specs=row,
        out_shape=jax.ShapeDtypeStruct((t, d), F32),
        scratch_shapes=[pltpu.VMEM((tm, d), F32)],
        compiler_params=_cparams(("parallel", "arbitrary")),
        name="ffn",
    )(h2d, w1.astype(BF16), w3.astype(BF16), w2.astype(BF16), x2d, gt)


def _project_inputs(h2d, w_in, na_qnorm_g, na_knorm_g, da_qnorm_g, da_knorm_g):
    segments = ((2 * NA_W, "rms128"), (NA_W, "bf16"), (2 * DA_QK_W, "rms64"), (DA_V_W, "bf16"),
                (2 * GDN_K_W + 2 * GDN_V_W, "f32"))
    roles = [role for width, role in segments for _ in range(width // PROJ_TN)]
    n_main = sum(width for width, _ in segments)
    gain = jnp.concatenate([jnp.tile(na_qnorm_g * HEAD_DIM ** -0.5, NA_HEADS),
                            jnp.tile(na_knorm_g, NA_HEADS),
                            jnp.ones((NA_W,), F32),
                            jnp.tile(da_qnorm_g * (LOG2E * DA_QK_DIM ** -0.5), 2 * DA_HEADS),
                            jnp.tile(da_knorm_g, 2 * DA_HEADS),
                            jnp.ones((DA_V_W + 2 * GDN_K_W + 2 * GDN_V_W,), F32)])
    attn, gdn_u = _proj_mixers(h2d, w_in[:, :n_main], gain, roles)
    w_lg = jnp.pad(w_in[:, n_main:], ((0, 0), (0, LANES - 4 * GDN_HEADS)))
    return attn, gdn_u, _proj(h2d, w_lg, F32)


def _layer(l, x, mod, norm1_g, norm2_g, w_in, gate_w, gate_b, na_qnorm_g, na_knorm_g, na_rpb,
           da_qnorm_g, da_knorm_g, da_lambda, da_subln_g, gdn_conv_w, gdn_a_log, gdn_dt_bias,
           gdn_norm_g, w_branch_a, w_branch_b, w_branch_c, w_out, ffn_w1, ffn_w3, ffn_w2):
    b, s, d = x.shape
    t = b * s
    sh1, sc1, gt1, sh2, sc2, gt2 = [mod[:, i * d:(i + 1) * d] for i in range(6)]

    h = _norm_mod(x, norm1_g, sc1, sh1)
    h2d = h.reshape(t, d)

    attn, gdn_u, gd_lg = _project_inputs(h2d, w_in, na_qnorm_g, na_knorm_g, da_qnorm_g, da_knorm_g)
    attn = attn.reshape(b, s, -1)

    ya = _natten(attn, na_rpb)
    lam_init = 0.8 - 0.6 * math.exp(-0.3 * l)
    da0 = 3 * NA_W
    yb = _diff_attn(attn[:, :, da0:da0 + 2 * DA_QK_W], attn[:, :, da0 + 2 * DA_QK_W:],
                    da_lambda, da_subln_g, lam_init)
    yc = _gated_deltanet(gdn_u.reshape(b, s, -1), gd_lg.reshape(b, s, -1), gdn_conv_w, gdn_a_log,
                         gdn_dt_bias, gdn_norm_g)

    x2d = _merge(h2d, ya.reshape(t, -1), yb.reshape(t, -1), yc.reshape(t, -1),
                 gate_w, gate_b, w_branch_a, w_branch_b, w_branch_c, w_out,
                 x.reshape(t, d), gt1.reshape(b, 1, d), s)

    h2 = _norm_mod(x2d.reshape(b, s, d), norm2_g, sc2, sh2)
    x2d = _ffn(h2.reshape(t, d), ffn_w1, ffn_w3, ffn_w2, x2d, gt2.reshape(b, 1, d), s)
    return x2d.reshape(b, s, d)


def kernel(x, c, ada_w, ada_b, norm1_g, norm2_g, w_in, gate_w, gate_b, na_qnorm_g, na_knorm_g, na_rpb, da_qnorm_g, da_knorm_g, da_lambda, da_subln_g, gdn_conv_w, gdn_a_log, gdn_dt_bias, gdn_norm_g, w_branch_a, w_branch_b, w_branch_c, w_out, ffn_w1, ffn_w3, ffn_w2):
    depth = ada_w.shape[0]
    mod = _adaln(c, ada_w, ada_b)
    per_layer = (norm1_g, norm2_g, w_in, gate_w, gate_b, na_qnorm_g, na_knorm_g, na_rpb,
                 da_qnorm_g, da_knorm_g, da_lambda, da_subln_g, gdn_conv_w, gdn_a_log,
                 gdn_dt_bias, gdn_norm_g, w_branch_a, w_branch_b, w_branch_c, w_out,
                 ffn_w1, ffn_w3, ffn_w2)
    for l in range(depth):
        x = _layer(l, x, mod[l], *[p[l] for p in per_layer])
    return x
```

```python
import functools
import math
from typing import NamedTuple

import jax
import jax.numpy as jnp
from jax import lax
from jax.experimental import pallas as pl
from jax.experimental.pallas import tpu as pltpu

F32 = jnp.float32
BF16 = jnp.bfloat16

GRID_W = 64
HEAD_DIM = 128
NA_HEADS = 4
NA_WIN_R = 8
NA_WIN_C = 16
DA_HEADS = 4
DA_QK_DIM = 64
DA_V_DIM = 128
GDN_HEADS = 8
GDN_DK = 128
GDN_DV = 128
GDN_CONV = 5
GDN_CHUNK = 64
GDN_SUPER = 256
N_BRANCHES = 3
RMS_EPS = 1e-6

NA_W = NA_HEADS * HEAD_DIM
DA_QK_W = DA_HEADS * 2 * DA_QK_DIM
DA_V_W = DA_HEADS * DA_V_DIM
GDN_K_W = GDN_HEADS * GDN_DK
GDN_V_W = GDN_HEADS * GDN_DV

LANES = 128
VMEM_LIMIT = 56 * 1024 * 1024
NEG_BIG = -1e30

_LOG2_CHUNK = GDN_CHUNK.bit_length() - 1

_NT = (((1,), (1,)), ((), ()))


def _cparams(sem):
    return pltpu.CompilerParams(dimension_semantics=sem, vmem_limit_bytes=VMEM_LIMIT)


def _sigmoid(x):
    return 1.0 / (1.0 + jnp.exp(-x))


def _dot(a, b):
    return jnp.dot(a.astype(BF16), b.astype(BF16), preferred_element_type=F32)


def _dot_nt(a, b):
    return lax.dot_general(a.astype(BF16), b.astype(BF16), _NT, preferred_element_type=F32)


def _mod_kernel(c_ref, w_ref, b_ref, o_ref):
    c = c_ref[...]
    a = c * _sigmoid(c)
    o_ref[0] = jnp.dot(a, w_ref[0], preferred_element_type=F32,
                       precision=lax.Precision.HIGHEST) + b_ref[0]


def _adaln(c, ada_w, ada_b):
    depth, d, n = ada_w.shape
    b = c.shape[0]
    rows = 8
    cp = jnp.zeros((rows, d), F32).at[:b].set(c)
    tn = 1024
    out = pl.pallas_call(
        _mod_kernel,
        grid=(depth, n // tn),
        in_specs=[pl.BlockSpec((rows, d), lambda l, j: (0, 0)),
                  pl.BlockSpec((1, d, tn), lambda l, j: (l, 0, j)),
                  pl.BlockSpec((1, 1, tn), lambda l, j: (l, 0, j))],
        out_specs=pl.BlockSpec((1, rows, tn), lambda l, j: (l, 0, j)),
        out_shape=jax.ShapeDtypeStruct((depth, rows, n), F32),
        compiler_params=_cparams(("arbitrary", "arbitrary")),
        name="adaln_mod",
    )(cp, ada_w, ada_b.reshape(depth, 1, n))
    return out[:, :b]


def _norm_kernel(x_ref, g_ref, sc_ref, sh_ref, o_ref):
    x = x_ref[0]
    y = x * lax.rsqrt(jnp.mean(x * x, axis=-1, keepdims=True) + RMS_EPS) * g_ref[...]
    o_ref[0] = (y * (1.0 + sc_ref[0]) + sh_ref[0]).astype(o_ref.dtype)


def _norm_mod(x, g, sc, sh):
    b, s, d = x.shape
    ts = min(512, s)
    return pl.pallas_call(
        _norm_kernel,
        grid=(b, s // ts),
        in_specs=[pl.BlockSpec((1, ts, d), lambda i, j: (i, j, 0)),
                  pl.BlockSpec((1, d), lambda i, j: (0, 0)),
                  pl.BlockSpec((1, 1, d), lambda i, j: (i, 0, 0)),
                  pl.BlockSpec((1, 1, d), lambda i, j: (i, 0, 0))],
        out_specs=pl.BlockSpec((1, ts, d), lambda i, j: (i, j, 0)),
        out_shape=jax.ShapeDtypeStruct((b, s, d), BF16),
        compiler_params=_cparams(("parallel", "parallel")),
        name="norm_mod",
    )(x, g.reshape(1, d), sc.reshape(b, 1, d), sh.reshape(b, 1, d))


def _group_rms_store(acc, gain_ref, o_ref, group):
    tn = acc.shape[1]
    for s in range(tn // LANES):
        sl = slice(s * LANES, (s + 1) * LANES)
        a = acc[:, sl]
        sq = a * a
        if group == LANES:
            ms = jnp.sum(sq, axis=-1, keepdims=True) * (1.0 / LANES)
        else:
            lo = lax.broadcasted_iota(jnp.int32, a.shape, 1) < group
            s_lo = jnp.sum(jnp.where(lo, sq, 0.0), axis=-1, keepdims=True)
            s_hi = jnp.sum(jnp.where(lo, 0.0, sq), axis=-1, keepdims=True)
            ms = jnp.where(lo, s_lo, s_hi) * (1.0 / group)
        y = a * lax.rsqrt(ms + RMS_EPS) * gain_ref[:, sl]
        o_ref[:, sl] = y.astype(o_ref.dtype)


def _proj_kernel(h_ref, w_ref, o_ref):
    o_ref[...] = jnp.dot(h_ref[...], w_ref[...], preferred_element_type=F32).astype(o_ref.dtype)


def _proj(h2d, w, out_dtype):
    t, d = h2d.shape
    n = w.shape[1]
    tm = min(1024, t)
    tn = min(512, n)
    return pl.pallas_call(
        _proj_kernel,
        grid=(t // tm, n // tn),
        in_specs=[pl.BlockSpec((tm, d), lambda i, j: (i, 0)),
                  pl.BlockSpec((d, tn), lambda i, j: (0, j))],
        out_specs=pl.BlockSpec((tm, tn), lambda i, j: (i, j)),
        out_shape=jax.ShapeDtypeStruct((t, n), out_dtype),
        compiler_params=_cparams(("parallel", "arbitrary")),
        name="proj",
    )(h2d, w.astype(BF16))


PROJ_TN = 512


def _proj_mixers_kernel(h_ref, w_ref, gain_ref, attn_ref, gdn_ref, *, roles):
    j = pl.program_id(1)
    acc = jnp.dot(h_ref[...], w_ref[...], preferred_element_type=F32)

    def tiles_with(role):
        cond = None
        for n, r in enumerate(roles):
            if r == role:
                cond = (j == n) if cond is None else (cond | (j == n))
        return cond

    @pl.when(tiles_with("rms128"))
    def _():
        _group_rms_store(acc, gain_ref, attn_ref, LANES)

    @pl.when(tiles_with("rms64"))
    def _():
        _group_rms_store(acc, gain_ref, attn_ref, LANES // 2)

    @pl.when(tiles_with("bf16"))
    def _():
        attn_ref[...] = acc.astype(attn_ref.dtype)

    @pl.when(tiles_with("f32"))
    def _():
        gdn_ref[...] = acc


def _proj_mixers(h2d, w, gain, roles):
    t, d = h2d.shape
    n = w.shape[1]
    tm = min(1024, t)
    tn = PROJ_TN
    n_attn = sum(r != "f32" for r in roles)
    assert len(roles) * tn == n and all(r == "f32" for r in roles[n_attn:])
    return pl.pallas_call(
        functools.partial(_proj_mixers_kernel, roles=tuple(roles)),
        grid=(t // tm, len(roles)),
        in_specs=[pl.BlockSpec((tm, d), lambda i, j: (i, 0)),
                  pl.BlockSpec((d, tn), lambda i, j: (0, j)),
                  pl.BlockSpec((1, tn), lambda i, j: (0, j))],
        out_specs=[pl.BlockSpec((tm, tn), lambda i, j: (i, jnp.minimum(j, n_attn - 1))),
                   pl.BlockSpec((tm, tn), lambda i, j: (i, jnp.maximum(j - n_attn, 0)))],
        out_shape=[jax.ShapeDtypeStruct((t, n_attn * tn), BF16),
                   jax.ShapeDtypeStruct((t, n - n_attn * tn), F32)],
        compiler_params=_cparams(("parallel", "arbitrary")),
        name="proj_mixers",
    )(h2d, w.astype(BF16), gain.reshape(1, n).astype(F32))


NA_ROWS_PER_STEP = 8
NA_BAND = NA_WIN_R * GRID_W


def _na_bias_table(rpb):
    h = rpb.shape[0]
    w = jnp.arange(GRID_W)
    col_start = jnp.clip(w - NA_WIN_C // 2, 0, GRID_W - NA_WIN_C)
    ck = jnp.arange(GRID_W)
    inwin = (ck[None, :] >= col_start[:, None]) & (ck[None, :] < col_start[:, None] + NA_WIN_C)
    cbi = jnp.clip(ck[None, :] - w[:, None] + (NA_WIN_C - 1), 0, 2 * NA_WIN_C - 2)
    e = jnp.arange(NA_WIN_R)
    j = jnp.arange(NA_WIN_R)
    rbi = j[None, :] + (NA_WIN_R - 1) - e[:, None]
    tab = rpb[:, rbi]
    tab = tab[:, :, :, cbi]
    tab = jnp.where(inwin[None, None, None], tab, NEG_BIG)
    return tab.transpose(1, 0, 3, 2, 4).reshape(NA_WIN_R, h, GRID_W, NA_BAND).astype(F32)


def _na_kernel(q_ref, kp_ref, kc_ref, kn_ref, vp_ref, vc_ref, vn_ref, tab_ref, o_ref,
               kbuf, vbuf, *, rows):
    rb = pl.program_id(1)
    blk = NA_ROWS_PER_STEP * GRID_W
    for n, (kr, vr) in enumerate(((kp_ref, vp_ref), (kc_ref, vc_ref), (kn_ref, vn_ref))):
        kbuf[n * blk:(n + 1) * blk, :] = kr[0]
        vbuf[n * blk:(n + 1) * blk, :] = vr[0]
    heads = [slice(h * HEAD_DIM, (h + 1) * HEAD_DIM) for h in range(NA_HEADS)]

    def band(i):
        r = rb * NA_ROWS_PER_STEP + i
        r_start = jnp.clip(r - NA_WIN_R // 2, 0, rows - NA_WIN_R)
        loc = pl.multiple_of((r_start - (rb - 1) * NA_ROWS_PER_STEP) * GRID_W, GRID_W)
        return r - r_start, loc, slice(i * GRID_W, (i + 1) * GRID_W)

    def scores(i):
        e, loc, rs = band(i)
        return [lax.dot_general(q_ref[0, rs, hs], kbuf[pl.ds(loc, NA_BAND), hs], _NT,
                                preferred_element_type=F32) + tab_ref[e, h]
                for h, hs in enumerate(heads)]

    def finish(i, s_heads):
        _, loc, rs = band(i)
        for hs, s in zip(heads, s_heads):
            p = jnp.exp(s - jnp.max(s, axis=-1, keepdims=True))
            l = jnp.sum(p, axis=-1, keepdims=True)
            o = jnp.dot(p.astype(BF16), vbuf[pl.ds(loc, NA_BAND), hs], preferred_element_type=F32) / l
            o_ref[0, rs, hs] = o.astype(o_ref.dtype)

    pending = scores(0)
    for i in range(1, NA_ROWS_PER_STEP):
        nxt = scores(i)
        finish(i - 1, pending)
        pending = nxt
    finish(NA_ROWS_PER_STEP - 1, pending)


def _natten(qkv, rpb):
    b, s, _ = qkv.shape
    rows = s // GRID_W
    blk = NA_ROWS_PER_STEP * GRID_W
    nrb = s // blk
    tab = _na_bias_table(rpb)

    def at(off, col):
        return lambda i, j: (i, jnp.clip(j + off, 0, nrb - 1), col)

    return pl.pallas_call(
        functools.partial(_na_kernel, rows=rows),
        grid=(b, nrb),
        in_specs=[pl.BlockSpec((1, blk, NA_W), at(0, 0)),
                  pl.BlockSpec((1, blk, NA_W), at(-1, 1)),
                  pl.BlockSpec((1, blk, NA_W), at(0, 1)),
                  pl.BlockSpec((1, blk, NA_W), at(1, 1)),
                  pl.BlockSpec((1, blk, NA_W), at(-1, 2)),
                  pl.BlockSpec((1, blk, NA_W), at(0, 2)),
                  pl.BlockSpec((1, blk, NA_W), at(1, 2)),
                  pl.BlockSpec(tab.shape, lambda i, j: (0, 0, 0, 0))],
        out_specs=pl.BlockSpec((1, blk, NA_W), at(0, 0)),
        out_shape=jax.ShapeDtypeStruct((b, s, NA_W), BF16),
        scratch_shapes=[pltpu.VMEM((3 * blk, NA_W), BF16), pltpu.VMEM((3 * blk, NA_W), BF16)],
        compiler_params=_cparams(("parallel", "parallel")),
        name="natten",
    )(*([qkv] * 7), tab)


DA_TILE = 512
DA_BLOCK = 256
DA_HEADS_PER_STEP = 2
LOG2E = math.log2(math.e)


def _da_slopes():
    return [LOG2E * 2.0 ** (-8.0 * (i + 1) / DA_HEADS) for i in range(DA_HEADS)]


def _bf16_parts(x):
    p1 = x.astype(BF16)
    r1 = x - p1.astype(F32)
    p2 = r1.astype(BF16)
    p3 = (r1 - p2.astype(F32)).astype(BF16)
    return [p1, p2, p3]


def _da_pos_tables(tile):
    slopes = jnp.asarray(_da_slopes(), F32)[:, None]
    pos = jnp.arange(tile)
    parts = _bf16_parts(slopes * pos.astype(F32)[None, :])
    ones = [jnp.ones((DA_HEADS, tile), BF16)] * 3
    zpad = jnp.zeros((DA_HEADS, DA_QK_DIM - 6, tile), BF16)
    kfeat = jnp.concatenate([jnp.stack(parts + ones, axis=1), zpad], axis=1)
    qfeat = jnp.concatenate([jnp.stack(ones + [-p for p in parts], axis=1), zpad], axis=1)
    dist = jnp.abs(pos[:, None] - pos[None, :]).astype(F32)
    bdiag = -slopes[:, :, None] * dist[None]
    return qfeat, kfeat, bdiag


def _da_kernel(qt_ref, ka_ref, vt_ref, qpos_ref, bdiag_ref, slope_ref, lam_ref, g_ref,
               o_ref, qa_sc, st0_sc, st1_sc, mt0_sc, mt1_sc, m_sc, l_sc, acc_sc, *, tile, lam_init):
    hg = pl.program_id(1)
    qi = pl.program_id(2)
    nk = vt_ref.shape[2]
    half = DA_QK_DIM
    nprob = 2 * DA_HEADS_PER_STEP
    slopes =[slope_ref[hg * DA_HEADS_PER_STEP + hh] for hh in range(DA_HEADS_PER_STEP)]

    for hh in range(DA_HEADS_PER_STEP):
        qt = qt_ref[0, hh * LANES:(hh + 1) * LANES]
        qpos = qpos_ref[hh]
        for sel, pp in enumerate((qpos, -qpos, jnp.zeros_like(qpos))):
            qa_sc[sel, 2 * hh] = jnp.concatenate([qt[0:half], pp], axis=0)
            qa_sc[sel, 2 * hh + 1] = jnp.concatenate([pp, qt[half:2 * half]], axis=0)

    def keys(p, start, size):
        return ka_ref[0, p // 2, p % 2, pl.ds(start, size), :]

    def values(p, t):
        return vt_ref[0, p // 2, t]

    m_sc[...] = jnp.full(m_sc.shape, NEG_BIG, F32)
    acc_sc[...] = jnp.zeros(acc_sc.shape, F32)
    l_sc[...] = jnp.zeros(l_sc.shape, F32)
    blocks = [slice(i, i + DA_BLOCK) for i in range(0, tile, DA_BLOCK)]

    def stage_tile(s):
        after = (s - 1 >= qi).astype(jnp.int32)
        return s - 1 + after, after

    def score_units(t, sel, st_sc, mt_sc, diag=False):
        for p in range(nprob):
            for qs in blocks:
                mcol = None
                for ks in blocks:
                    start = pl.multiple_of(t * tile + ks.start, DA_BLOCK)
                    st = jnp.dot(keys(p, start, DA_BLOCK), qa_sc[sel, p, :, qs],
                                 preferred_element_type=F32)
                    if diag:
                        st = st + bdiag_ref[p // 2, ks, qs]
                    st_sc[p, ks, qs] = st
                    mk = jnp.max(st, axis=0, keepdims=True)
                    mcol = mk if mcol is None else jnp.maximum(mcol, mk)
                    if ks is blocks[-1]:
                        mt_sc[p, :, qs] = mcol
                    yield

    def softmax_units(t, st_sc, mt_sc):
        dist = (tile * jnp.abs(t - qi)).astype(F32)
        stats = []
        for p in range(nprob):
            shift = -slopes[p // 2] * dist
            m_old = m_sc[p]
            m_new = jnp.maximum(m_old, mt_sc[p] + shift)
            stats.append((jnp.exp2(m_old - m_new), m_new - shift))
            m_sc[p] = m_new
        for p, (alpha, msub) in enumerate(stats):
            vat = values(p, t)
            for qs in blocks:
                acc = alpha[:, qs] * acc_sc[p, :, qs]
                den = alpha[:, qs] * l_sc[p, :, qs]
                for ks in blocks:
                    pt = jnp.exp2(st_sc[p, ks, qs] - msub[:, qs])
                    den = den + jnp.sum(pt, axis=0, keepdims=True)
                    acc = acc + jnp.dot(vat[:, ks], pt.astype(BF16), preferred_element_type=F32)
                    if ks is blocks[-1]:
                        acc_sc[p, :, qs] = acc
                        l_sc[p, :, qs] = den
                    yield

    def overlapped(scores, softmax):
        if scores is not None:
            next(scores)
        for _ in softmax:
            if scores is not None:
                next(scores, None)

    bufs = ((st0_sc, mt0_sc), (st1_sc, mt1_sc))
    overlapped(None, score_units(qi, 2, *bufs[0], diag=True))
    if nk == 1:
        overlapped(None, softmax_units(qi, *bufs[0]))
    else:
        overlapped(score_units(*stage_tile(1), *bufs[1]), softmax_units(qi, *bufs[0]))

        def body(s, carry):
            for parity in range(2):
                @pl.when((s & 1) == parity)
                def _():
                    overlapped(score_units(*stage_tile(s + 1), *bufs[1 - parity]),
                               softmax_units(stage_tile(s)[0], *bufs[parity]))
            return carry

        lax.fori_loop(1, nk - 1, body, 0)
        overlapped(None, softmax_units(stage_tile(nk - 1)[0], *bufs[(nk - 1) % 2]))

    lp = lam_ref[...]
    lam = (jnp.exp(jnp.sum(lp[0:1] * lp[1:2], axis=-1, keepdims=True))
           - jnp.exp(jnp.sum(lp[2:3] * lp[3:4], axis=-1, keepdims=True)) + lam_init)
    for hh in range(DA_HEADS_PER_STEP):
        o0 = acc_sc[2 * hh] / l_sc[2 * hh]
        o1 = acc_sc[2 * hh + 1] / l_sc[2 * hh + 1]
        a = o0 - lam * o1
        y = a * lax.rsqrt(jnp.mean(a * a, axis=0, keepdims=True) + RMS_EPS) * g_ref[...]
        o_ref[0, :, hh * DA_V_DIM:(hh + 1) * DA_V_DIM] = (y * (1.0 - lam_init)).T.astype(o_ref.dtype)


def _diff_attn(qk, v, lam_p, subln_g, lam_init):
    b, s, _ = v.shape
    tile = min(DA_TILE, s)
    nt = s // tile
    nh = DA_HEADS
    qfeat, kfeat, bdiag = _da_pos_tables(tile)
    slopes = jnp.asarray(_da_slopes(), F32)

    qt = qk[:, :, :DA_QK_W].transpose(0, 2, 1)
    k4 = qk[:, :, DA_QK_W:].reshape(b, s, nh, 2, DA_QK_DIM)
    kp = jnp.broadcast_to(jnp.tile(kfeat.transpose(0, 2, 1), (1, nt, 1))[None], (b, nh, s, DA_QK_DIM))
    k0 = jnp.concatenate([k4[:, :, :, 0].transpose(0, 2, 1, 3), kp], axis=-1)
    k1 = jnp.concatenate([kp, k4[:, :, :, 1].transpose(0, 2, 1, 3)], axis=-1)
    ka = jnp.stack([k0, k1], axis=2)
    vt = v.reshape(b, nt, tile, nh, DA_V_DIM).transpose(0, 3, 1, 4, 2)

    hps = DA_HEADS_PER_STEP
    nprob = 2 * hps
    resident = pl.Buffered(1)
    return pl.pallas_call(
        functools.partial(_da_kernel, tile=tile, lam_init=lam_init),
        grid=(b, nh // hps, nt),
        in_specs=[pl.BlockSpec((1, hps * LANES, tile), lambda i, h, q: (i, h, q)),
                  pl.BlockSpec((1, hps, 2, s, LANES), lambda i, h, q: (i, h, 0, 0, 0),
                               pipeline_mode=resident),
                  pl.BlockSpec((1, hps, nt, DA_V_DIM, tile), lambda i, h, q: (i, h, 0, 0, 0),
                               pipeline_mode=resident),
                  pl.BlockSpec((hps, DA_QK_DIM, tile), lambda i, h, q: (h, 0, 0)),
                  pl.BlockSpec((hps, tile, tile), lambda i, h, q: (h, 0, 0)),
                  pl.BlockSpec(memory_space=pltpu.SMEM),
                  pl.BlockSpec((4, DA_QK_DIM), lambda i, h, q: (0, 0)),
                  pl.BlockSpec((DA_V_DIM, 1), lambda i, h, q: (0, 0))],
        out_specs=pl.BlockSpec((1, tile, hps * DA_V_DIM), lambda i, h, q: (i, q, h)),
        out_shape=jax.ShapeDtypeStruct((b, s, DA_V_W), BF16),
        scratch_shapes=[pltpu.VMEM((3, nprob, LANES, tile), BF16),
                        pltpu.VMEM((nprob, tile, tile), F32),
                        pltpu.VMEM((nprob, tile, tile), F32),
                        pltpu.VMEM((nprob, 1, tile), F32),
                        pltpu.VMEM((nprob, 1, tile), F32),
                        pltpu.VMEM((nprob, 1, tile), F32),
                        pltpu.VMEM((nprob, 1, tile), F32),
                        pltpu.VMEM((nprob, DA_V_DIM, tile), F32)],
        compiler_params=_cparams(("parallel", "parallel", "arbitrary")),
        name="diff_attn",
    )(qt, ka, vt, qfeat, bdiag, slopes, lam_p.astype(F32),
      subln_g.reshape(DA_V_DIM, 1).astype(F32))


def _gdn_gate_kernel(lg_ref, alog_ref, dt_ref, beta_ref, gc_ref, gt_ref):
    lg = lg_ref[0]
    nh = 2 * GDN_HEADS
    beta_ref[0] = _sigmoid(lg[0:nh])
    z = lg[nh:2 * nh] + dt_ref[...]
    sp = jnp.maximum(z, 0.0) + jnp.log(1.0 + jnp.exp(-jnp.abs(z)))
    g = -jnp.exp(alog_ref[...]) * sp
    ts = g.shape[1]
    pos = lax.broadcasted_iota(jnp.int32, g.shape, 1) & (GDN_CHUNK - 1)
    pre = g
    suf = g
    sh = 1
    while sh < GDN_CHUNK:
        pre = pre + jnp.where(pos >= sh, pltpu.roll(pre, sh, axis=1), 0.0)
        suf = suf + jnp.where(pos < GDN_CHUNK - sh, pltpu.roll(suf, ts - sh, axis=1), 0.0)
        sh *= 2
    bwd = lax.broadcasted_iota(jnp.int32, g.shape, 0) >= GDN_HEADS
    gc_ref[0] = jnp.where(bwd, suf, pre)
    gt_ref[0] = jnp.where(bwd, pre, suf) - g


def _gdn_gates(lgt, a_log, dt_bias):
    b, _, s = lgt.shape
    nh = 2 * GDN_HEADS
    ts = min(2048, s)
    shp = jax.ShapeDtypeStruct((b, nh, s), F32)
    spec = pl.BlockSpec((1, nh, ts), lambda i, j: (i, 0, j))
    return pl.pallas_call(
        _gdn_gate_kernel,
        grid=(b, s // ts),
        in_specs=[pl.BlockSpec((1, 2 * nh, ts), lambda i, j: (i, 0, j)),
                  pl.BlockSpec((nh, 1), lambda i, j: (0, 0)),
                  pl.BlockSpec((nh, 1), lambda i, j: (0, 0))],
        out_specs=[spec, spec, spec],
        out_shape=[shp, shp, shp],
        compiler_params=_cparams(("parallel", "parallel")),
        name="gdn_gates",
    )(lgt, a_log.reshape(nh, 1).astype(F32), dt_bias.reshape(nh, 1).astype(F32))


GDN_HALO = 8


def _gdn_conv_kernel(x_ref, p_ref, n_ref, w_ref, o_ref, ext, *, ts):
    cg = pl.program_id(0)
    i = pl.program_id(2)
    last = pl.num_programs(2) - 1
    ext[0:GDN_HALO, :] = jnp.where(i > 0, p_ref[0], 0.0)
    ext[GDN_HALO:GDN_HALO + ts, :] = x_ref[0]
    ext[GDN_HALO + ts:2 * GDN_HALO + ts, :] = jnp.where(i < last, n_ref[0], 0.0)
    y = None
    for tap in range(GDN_CONV):
        off = GDN_HALO + tap - GDN_CONV // 2
        term = ext[off:off + ts, :] * w_ref[0, tap:tap + 1, :]
        y = term if y is None else y + term
    y = y * _sigmoid(y)
    qscale = jnp.where(cg == 0, GDN_DK ** -0.5, 1.0).astype(F32)
    is_v = cg == 2
    for h in range(GDN_HEADS):
        hs = slice(h * GDN_DK, (h + 1) * GDN_DK)
        a = y[:, hs]
        nrm = lax.rsqrt(jnp.sum(a * a, axis=-1, keepdims=True) + RMS_EPS) * qscale
        o_ref[0, 0, :, hs] = a * jnp.where(is_v, 1.0, nrm)


def _gdn_conv(u, conv_w):
    b, s, _ = u.shape
    cw = GDN_K_W
    ts = min(512, s)
    nb = s // GDN_HALO
    per = ts // GDN_HALO
    w3 = conv_w.reshape(GDN_CONV, 3, cw).transpose(1, 0, 2)
    w3 = jnp.pad(w3, ((0, 0), (0, 8 - GDN_CONV), (0, 0))).astype(F32)
    return pl.pallas_call(
        functools.partial(_gdn_conv_kernel, ts=ts),
        grid=(3, b, s // ts),
        in_specs=[pl.BlockSpec((1, ts, cw), lambda c, i, j: (i, j, c)),
                  pl.BlockSpec((1, GDN_HALO, cw),
                               lambda c, i, j: (i, jnp.maximum(j * per - 1, 0), c)),
                  pl.BlockSpec((1, GDN_HALO, cw),
                               lambda c, i, j: (i, jnp.minimum((j + 1) * per, nb - 1), c)),
                  pl.BlockSpec((1, 8, cw), lambda c, i, j: (c, 0, 0))],
        out_specs=pl.BlockSpec((1, 1, ts, cw), lambda c, i, j: (c, i, j, 0)),
        out_shape=jax.ShapeDtypeStruct((3, b, s, cw), F32),
        scratch_shapes=[pltpu.VMEM((ts + 2 * GDN_HALO, cw), F32)],
        compiler_params=_cparams(("parallel", "parallel", "parallel")),
        name="gdn_conv",
    )(u, u, u, w3)


class _GdnChain(NamedTuple):
    q: jax.Array
    k: jax.Array
    v: jax.Array
    cols: jax.Array
    gr: jax.Array
    rev: bool
    state: object
    out: object


def _gdn_chains(chains):
    sc = GDN_SUPER
    ck = GDN_CHUNK
    nchunk = sc // ck
    row = lax.broadcasted_iota(jnp.int32, (ck, sc), 0)
    lane = lax.broadcasted_iota(jnp.int32, (ck, sc), 1)
    col = lane & (ck - 1)
    lane_chunk = lane >> _LOG2_CHUNK
    eye = jnp.where(row == col, 1.0, 0.0).astype(F32)
    masks = {False: (row >= col, row > col), True: (row <= col, row < col)}
    ii = lax.broadcasted_iota(jnp.int32, (sc, sc), 0)
    jj = lax.broadcasted_iota(jnp.int32, (sc, sc), 1)
    bd_mask = jnp.where((ii >> _LOG2_CHUNK) == (jj >> _LOG2_CHUNK), 1.0, 0.0).astype(BF16)

    def pack(full):
        out = full[0:ck]
        for b in range(1, nchunk):
            out = jnp.where(lane_chunk == b, full[b * ck:(b + 1) * ck], out)
        return out

    def blockdiag(packed_bf16):
        return jnp.concatenate([packed_bf16] * nchunk, axis=0) * bd_mask

    def lanes(c):
        return jnp.broadcast_to(c, (sc, LANES))

    bcast = [(lanes(ch.cols[:, 0:1]), lanes(ch.cols[:, 1:2]), lanes(ch.cols[:, 2:3])) for ch in chains]

    decays, kbs, a_mats = [], [], []
    for ch, (beta, gc, _) in zip(chains, bcast):
        incl, strict = masks[ch.rev]
        gc2 = jnp.concatenate([gc, gc], axis=1)
        decay = jnp.exp(jnp.where(incl, pack(gc2) - ch.gr, NEG_BIG))
        kb = ch.k * beta
        decays.append(decay)
        kbs.append(kb)
        a_mats.append(jnp.where(strict, pack(_dot_nt(kb, ch.k)) * decay, 0.0))

    xs = [-a for a in a_mats]
    ts = [eye + x for x in xs]
    xs = [_dot(x, blockdiag(x.astype(BF16))) for x in xs]
    span = 2
    while span < ck:
        last = 2 * span >= ck
        new_ts, new_xs = [], []
        for t, x in zip(ts, xs):
            w_bd = blockdiag(x.astype(BF16))
            if last:
                new_ts.append(t + _dot(t, w_bd))
                new_xs.append(None)
            else:
                both = _dot(jnp.concatenate([t, x], axis=0), w_bd)
                new_ts.append(t + both[0:ck])
                new_xs.append(both[ck:2 * ck])
        ts, xs = new_ts, new_xs
        span *= 2
    resids = []
    for a, t in zip(a_mats, ts):
        a_hi = a.astype(BF16)
        a_lo = (a - a_hi.astype(F32)).astype(BF16)
        t_hi = t.astype(BF16)
        t_lo = (t - t_hi.astype(F32)).astype(BF16)
        hi = jnp.dot(jnp.concatenate([a_hi, a_lo], axis=0), blockdiag(t_hi), preferred_element_type=F32)
        a_t = hi[0:ck] + hi[ck:2 * ck] + jnp.dot(a_hi, blockdiag(t_lo), preferred_element_type=F32)
        resids.append(eye - t - a_t)
    ts = [t + _dot(t, blockdiag(r.astype(BF16))) for t, r in zip(ts, resids)]

    pre = []
    for ch, (beta, gc, gt), decay, kb, t in zip(chains, bcast, decays, kbs, ts):
        egc = jnp.exp(gc)
        uw = _dot(blockdiag(t.astype(BF16)), jnp.concatenate([ch.v * beta, kb * egc], axis=1))
        pre.append(dict(u=uw[:, 0:GDN_DV], w=uw[:, GDN_DV:], qg=ch.q * egc,
                        ai=pack(_dot_nt(ch.q, ch.k)) * decay, kt_t=(ch.k * jnp.exp(gt)).T,
                        gl=jnp.exp(gc + gt)))

    states = [ch.state[...] for ch in chains]
    outs = [[None] * nchunk for _ in chains]
    for step in range(nchunk):
        order = [nchunk - 1 - step if ch.rev else step for ch in chains]
        spans = [slice(j * ck, (j + 1) * ck) for j in order]
        from_state = [_dot(jnp.concatenate([p["w"][sl], p["qg"][sl]], axis=0), s_mat)
                      for p, sl, s_mat in zip(pre, spans, states)]
        v_news = [p["u"][sl] - r[0:ck] for p, sl, r in zip(pre, spans, from_state)]
        from_v = [_dot(jnp.concatenate([p["ai"][:, sl], p["kt_t"][:, sl]], axis=0), v_new)
                  for p, sl, v_new in zip(pre, spans, v_news)]
        for n, (p, j, r1, r2) in enumerate(zip(pre, order, from_state, from_v)):
            outs[n][j] = r1[ck:2 * ck] + r2[0:ck]
            states[n] = states[n] * p["gl"][j * ck:j * ck + 1, :] + r2[ck:]
    for ch, s_mat, o in zip(chains, states, outs):
        ch.state[...] = s_mat
        ch.out[...] = jnp.concatenate(o, axis=0)


GDN_HEADS_PER_STEP = 8


def _gdn_scan_kernel(qf_ref, kf_ref, vf_ref, cf_ref, rf_ref, qb_ref, kb_ref, vb_ref, cb_ref, rb_ref,
                     of_ref, ob_ref, state):
    hg = pl.program_id(1)

    @pl.when(pl.program_id(2) == 0)
    def _():
        state[...] = jnp.zeros(state.shape, F32)

    per_group = 3 * GDN_HEADS_PER_STEP
    shift = (LANES - per_group * hg) & (LANES - 1)
    cols_fb = (pltpu.roll(cf_ref[0, 0], shift, axis=1), pltpu.roll(cb_ref[0, 0], shift, axis=1))

    chains = []
    for i in range(GDN_HEADS_PER_STEP):
        hd = hg * GDN_HEADS_PER_STEP + i
        ks = slice(i * GDN_DK, (i + 1) * GDN_DK)
        vs = slice(i * GDN_DV, (i + 1) * GDN_DV)
        for rev, (q_ref, k_ref, v_ref, r_ref, o_ref) in enumerate(
                ((qf_ref, kf_ref, vf_ref, rf_ref, of_ref), (qb_ref, kb_ref, vb_ref, rb_ref, ob_ref))):
            chains.append(_GdnChain(
                q=q_ref[0, 0, :, ks], k=k_ref[0, 0, :, ks], v=v_ref[0, 0, :, vs],
                cols=cols_fb[rev][:, 3 * i:3 * i + 3],
                gr=r_ref[0, 0, pl.ds(hd, 1), :], rev=bool(rev),
                state=state.at[2 * i + rev], out=o_ref.at[0, :, vs]))
    _gdn_chains(chains)


def _gdn_scan(qkv, cols, rows):
    _, b, s, _ = qkv.shape
    sc = GDN_SUPER
    nsc = s // sc
    hps = GDN_HEADS_PER_STEP

    def specs(d):
        seq = (lambda n: nsc - 1 - n) if d else (lambda n: n)
        return [pl.BlockSpec((1, 1, sc, hps * GDN_DK), lambda i, h, n: (0, i, seq(n), h)),
                pl.BlockSpec((1, 1, sc, hps * GDN_DK), lambda i, h, n: (1, i, seq(n), h)),
                pl.BlockSpec((1, 1, sc, hps * GDN_DV), lambda i, h, n: (2, i, seq(n), h)),
                pl.BlockSpec((1, 1, sc, LANES), lambda i, h, n: (i, d, seq(n), 0)),
                pl.BlockSpec((1, 1, GDN_HEADS, sc), lambda i, h, n: (i, d, 0, seq(n)))]

    out_shape = jax.ShapeDtypeStruct((b, s, GDN_V_W), F32)
    return pl.pallas_call(
        _gdn_scan_kernel,
        grid=(b, GDN_HEADS // hps, nsc),
        in_specs=specs(0) + specs(1),
        out_specs=[pl.BlockSpec((1, sc, hps * GDN_DV), lambda i, h, n: (i, n, h)),
                   pl.BlockSpec((1, sc, hps * GDN_DV), lambda i, h, n: (i, nsc - 1 - n, h))],
        out_shape=[out_shape, out_shape],
        scratch_shapes=[pltpu.VMEM((2 * hps, GDN_DK, GDN_DV), F32)],
        compiler_params=_cparams(("parallel", "parallel", "arbitrary")),
        name="gdn_scan",
    )(qkv, qkv, qkv, cols, rows, qkv, qkv, qkv, cols, rows)


def _gdn_out_kernel(of_ref, ob_ref, z_ref, g_ref, o_ref):
    o = of_ref[0] + ob_ref[0]
    z = z_ref[0].astype(F32)
    gate = z * _sigmoid(z)
    for h in range(GDN_HEADS):
        hs = slice(h * GDN_DV, (h + 1) * GDN_DV)
        a = o[:, hs]
        y = a * lax.rsqrt(jnp.mean(a * a, axis=-1, keepdims=True) + RMS_EPS) * g_ref[...]
        o_ref[0, :, hs] = (y * gate[:, hs]).astype(o_ref.dtype)


def _gdn_out(o_f, o_b, u, norm_g):
    b, s, w = o_f.shape
    ts = min(512, s)
    spec = pl.BlockSpec((1, ts, w), lambda i, j: (i, j, 0))
    z_spec = pl.BlockSpec((1, ts, w), lambda i, j: (i, j, 3))
    return pl.pallas_call(
        _gdn_out_kernel,
        grid=(b, s // ts),
        in_specs=[spec, spec, z_spec, pl.BlockSpec((1, GDN_DV), lambda i, j: (0, 0))],
        out_specs=spec,
        out_shape=jax.ShapeDtypeStruct((b, s, w), BF16),
        compiler_params=_cparams(("parallel", "parallel")),
        name="gdn_out",
    )(o_f, o_b, u, norm_g.reshape(1, GDN_DV).astype(F32))


def _gated_deltanet(u, lg, conv_w, a_log, dt_bias, norm_g):
    b, s, _ = u.shape
    nh = GDN_HEADS
    lgt = lg[:, :, :4 * nh].transpose(0, 2, 1)
    beta, gc, gt = _gdn_gates(lgt, a_log, dt_bias)
    rows = gc.reshape(b, 2, nh, s)
    cols = jnp.stack([beta, gc, gt], axis=-1).reshape(b, 2, nh, s, 3)
    cols = cols.transpose(0, 1, 3, 2, 4).reshape(b, 2, s, 3 * nh)
    cols = jnp.pad(cols, ((0, 0), (0, 0), (0, 0), (0, LANES - 3 * nh)))
    qkv = _gdn_conv(u, conv_w)
    o_f, o_b = _gdn_scan(qkv, cols, rows)
    return _gdn_out(o_f, o_b, u, norm_g)


def _merge_kernel(h_ref, ya_ref, yb_ref, yc_ref, gw0_ref, gw1_ref, gw2_ref,
                  gb0_ref, gb1_ref, gb2_ref, wa_ref, wb_ref, wc_ref, wo_ref,
                  x_ref, gt_ref, o_ref, acc):
    j = pl.program_id(1)
    h = h_ref[...]

    def gate(gw_ref, gb_ref):
        return _sigmoid(jnp.dot(h, gw_ref[...], preferred_element_type=F32) + gb_ref[...])

    m = (gate(gw0_ref, gb0_ref) * jnp.dot(ya_ref[...], wa_ref[...], preferred_element_type=F32)
         + gate(gw1_ref, gb1_ref) * jnp.dot(yb_ref[...], wb_ref[...], preferred_element_type=F32)
         + gate(gw2_ref, gb2_ref) * jnp.dot(yc_ref[...], wc_ref[...], preferred_element_type=F32))
    part = jnp.dot(m.astype(BF16), wo_ref[...], preferred_element_type=F32)

    @pl.when(j == 0)
    def _():
        acc[...] = part

    @pl.when(j > 0)
    def _():
        acc[...] += part

    @pl.when(j == pl.num_programs(1) - 1)
    def _():
        o_ref[...] = x_ref[...] + gt_ref[0] * acc[...]


def _merge(h2d, ya, yb, yc, gate_w, gate_b, wa, wb, wc, w_out, x2d, gt, seq):
    t, d = x2d.shape
    tm = min(512, seq)
    tn = 512
    nj = d // tn
    per_batch = seq // tm
    gw = gate_w.astype(BF16)
    gb = gate_b.reshape(1, N_BRANCHES * d).astype(F32)
    row = lambda w: pl.BlockSpec((tm, w), lambda i, j: (i, 0))
    gws = [pl.BlockSpec((d, tn), (lambda i, j, r=r: (0, r * nj + j))) for r in range(N_BRANCHES)]
    gbs = [pl.BlockSpec((1, tn), (lambda i, j, r=r: (0, r * nj + j))) for r in range(N_BRANCHES)]
    return pl.pallas_call(
        _merge_kernel,
        grid=(t // tm, nj),
        in_specs=[row(d), row(ya.shape[1]), row(yb.shape[1]), row(yc.shape[1]),
                  *gws, *gbs,
                  pl.BlockSpec((wa.shape[0], tn), lambda i, j: (0, j)),
                  pl.BlockSpec((wb.shape[0], tn), lambda i, j: (0, j)),
                  pl.BlockSpec((wc.shape[0], tn), lambda i, j: (0, j)),
                  pl.BlockSpec((tn, d), lambda i, j: (j, 0)),
                  row(d),
                  pl.BlockSpec((1, 1, d), lambda i, j: (i // per_batch, 0, 0))],
        out_specs=row(d),
        out_shape=jax.ShapeDtypeStruct((t, d), F32),
        scratch_shapes=[pltpu.VMEM((tm, d), F32)],
        compiler_params=_cparams(("parallel", "arbitrary")),
        name="merge",
    )(h2d, ya, yb, yc, gw, gw, gw, gb, gb, gb,
      wa.astype(BF16), wb.astype(BF16), wc.astype(BF16), w_out.astype(BF16), x2d, gt)


def _ffn_kernel(h_ref, w1_ref, w3_ref, w2_ref, x_ref, gt_ref, o_ref, acc):
    j = pl.program_id(1)
    h = h_ref[...]
    a = jnp.dot(h, w1_ref[...], preferred_element_type=F32)
    b = jnp.dot(h, w3_ref[...], preferred_element_type=F32)
    part = jnp.dot((a * _sigmoid(a) * b).astype(BF16), w2_ref[...], preferred_element_type=F32)

    @pl.when(j == 0)
    def _():
        acc[...] = part

    @pl.when(j > 0)
    def _():
        acc[...] += part

    @pl.when(j == pl.num_programs(1) - 1)
    def _():
        o_ref[...] = x_ref[...] + gt_ref[0] * acc[...]


def _ffn(h2d, w1, w3, w2, x2d, gt, seq):
    t, d = x2d.shape
    f = w1.shape[1]
    tm = min(512, seq)
    tf = 512
    per_batch = seq // tm
    row = pl.BlockSpec((tm, d), lambda i, j: (i, 0))
    return pl.pallas_call(
        _ffn_kernel,
        grid=(t // tm, f // tf),
        in_specs=[row,
                  pl.BlockSpec((d, tf), lambda i, j: (0, j)),
                  pl.BlockSpec((d, tf), lambda i, j: (0, j)),
                  pl.BlockSpec((tf, d), lambda i, j: (j, 0)),
                  row,
                  pl.BlockSpec((1, 1, d), lambda i, j: (i // per_batch, 0, 0))],
        out_specs=row,
        out_shape=jax.ShapeDtypeStruct((t, d), F32),
        scratch_shapes=[pltpu.VMEM((tm, d), F32)],
        compiler_params=_cparams(("parallel", "arbitrary")),
        name="ffn",
    )(h2d, w1.astype(BF16), w3.astype(BF16), w2.astype(BF16), x2d, gt)


def _project_inputs(h2d, w_in, na_qnorm_g, na_knorm_g, da_qnorm_g, da_knorm_g):
    segments = ((2 * NA_W, "rms128"), (NA_W, "bf16"), (2 * DA_QK_W, "rms64"), (DA_V_W, "bf16"),
                (2 * GDN_K_W + 2 * GDN_V_W, "f32"))
    roles = [role for width, role in segments for _ in range(width // PROJ_TN)]
    n_main = sum(width for width, _ in segments)
    gain = jnp.concatenate([jnp.tile(na_qnorm_g * HEAD_DIM ** -0.5, NA_HEADS),
                            jnp.tile(na_knorm_g, NA_HEADS),
                            jnp.ones((NA_W,), F32),
                            jnp.tile(da_qnorm_g * (LOG2E * DA_QK_DIM ** -0.5), 2 * DA_HEADS),
                            jnp.tile(da_knorm_g, 2 * DA_HEADS),
                            jnp.ones((DA_V_W + 2 * GDN_K_W + 2 * GDN_V_W,), F32)])
    attn, gdn_u = _proj_mixers(h2d, w_in[:, :n_main], gain, roles)
    w_lg = jnp.pad(w_in[:, n_main:], ((0, 0), (0, LANES - 4 * GDN_HEADS)))
    return attn, gdn_u, _proj(h2d, w_lg, F32)


def _layer(l, x, mod, norm1_g, norm2_g, w_in, gate_w, gate_b, na_qnorm_g, na_knorm_g, na_rpb,
           da_qnorm_g, da_knorm_g, da_lambda, da_subln_g, gdn_conv_w, gdn_a_log, gdn_dt_bias,
           gdn_norm_g, w_branch_a, w_branch_b, w_branch_c, w_out, ffn_w1, ffn_w3, ffn_w2):
    b, s, d = x.shape
    t = b * s
    sh1, sc1, gt1, sh2, sc2, gt2 = [mod[:, i * d:(i + 1) * d] for i in range(6)]

    h = _norm_mod(x, norm1_g, sc1, sh1)
    h2d = h.reshape(t, d)

    attn, gdn_u, gd_lg = _project_inputs(h2d, w_in, na_qnorm_g, na_knorm_g, da_qnorm_g, da_knorm_g)
    attn = attn.reshape(b, s, -1)

    ya = _natten(attn, na_rpb)
    lam_init = 0.8 - 0.6 * math.exp(-0.3 * l)
    da0 = 3 * NA_W
    yb = _diff_attn(attn[:, :, da0:da0 + 2 * DA_QK_W], attn[:, :, da0 + 2 * DA_QK_W:],
                    da_lambda, da_subln_g, lam_init)
    yc = _gated_deltanet(gdn_u.reshape(b, s, -1), gd_lg.reshape(b, s, -1), gdn_conv_w, gdn_a_log,
                         gdn_dt_bias, gdn_norm_g)

    x2d = _merge(h2d, ya.reshape(t, -1), yb.reshape(t, -1), yc.reshape(t, -1),
                 gate_w, gate_b, w_branch_a, w_branch_b, w_branch_c, w_out,
                 x.reshape(t, d), gt1.reshape(b, 1, d), s)

    h2 = _norm_mod(x2d.reshape(b, s, d), norm2_g, sc2, sh2)
    x2d = _ffn(h2.reshape(t, d), ffn_w1, ffn_w3, ffn_w2, x2d, gt2.reshape(b, 1, d), s)
    return x2d.reshape(b, s, d)


def kernel(x, c, ada_w, ada_b, norm1_g, norm2_g, w_in, gate_w, gate_b, na_qnorm_g, na_knorm_g, na_rpb, da_qnorm_g, da_knorm_g, da_lambda, da_subln_g, gdn_conv_w, gdn_a_log, gdn_dt_bias, gdn_norm_g, w_branch_a, w_branch_b, w_branch_c, w_out, ffn_w1, ffn_w3, ffn_w2):
    depth = ada_w.shape[0]
    mod = _adaln(c, ada_w, ada_b)
    per_layer = (norm1_g, norm2_g, w_in, gate_w, gate_b, na_qnorm_g, na_knorm_g, na_rpb,
                 da_qnorm_g, da_knorm_g, da_lambda, da_subln_g, gdn_conv_w, gdn_a_log,
                 gdn_dt_bias, gdn_norm_g, w_branch_a, w_branch_b, w_branch_c, w_out,
                 ffn_w1, ffn_w3, ffn_w2)
    for l in range(depth):
        x = _layer(l, x, mod[l], *[p[l] for p in per_layer])
    return x
```

```python
import functools
import math
from typing import NamedTuple

import jax
import jax.numpy as jnp
from jax import lax
from jax.experimental import pallas as pl
from jax.experimental.pallas import tpu as pltpu

F32 = jnp.float32
BF16 = jnp.bfloat16

GRID_W = 64
HEAD_DIM = 128
NA_HEADS = 4
NA_WIN_R = 8
NA_WIN_C = 16
DA_HEADS = 4
DA_QK_DIM = 64
DA_V_DIM = 128
GDN_HEADS = 8
GDN_DK = 128
GDN_DV = 128
GDN_CONV = 5
GDN_CHUNK = 64
GDN_SUPER = 256
N_BRANCHES = 3
RMS_EPS = 1e-6

NA_W = NA_HEADS * HEAD_DIM
DA_QK_W = DA_HEADS * 2 * DA_QK_DIM
DA_V_W = DA_HEADS * DA_V_DIM
GDN_K_W = GDN_HEADS * GDN_DK
GDN_V_W = GDN_HEADS * GDN_DV

LANES = 128
VMEM_LIMIT = 56 * 1024 * 1024
NEG_BIG = -1e30

_LOG2_CHUNK = GDN_CHUNK.bit_length() - 1

_NT = (((1,), (1,)), ((), ()))


def _cparams(sem):
    return pltpu.CompilerParams(dimension_semantics=sem, vmem_limit_bytes=VMEM_LIMIT)


def _sigmoid(x):
    return 1.0 / (1.0 + jnp.exp(-x))


def _dot(a, b):
    return jnp.dot(a.astype(BF16), b.astype(BF16), preferred_element_type=F32)


def _dot_nt(a, b):
    return lax.dot_general(a.astype(BF16), b.astype(BF16), _NT, preferred_element_type=F32)


def _mod_kernel(c_ref, w_ref, b_ref, o_ref):
    c = c_ref[...]
    a = c * _sigmoid(c)
    o_ref[0] = jnp.dot(a, w_ref[0], preferred_element_type=F32,
                       precision=lax.Precision.HIGHEST) + b_ref[0]


def _adaln(c, ada_w, ada_b):
    depth, d, n = ada_w.shape
    b = c.shape[0]
    rows = 8
    cp = jnp.zeros((rows, d), F32).at[:b].set(c)
    tn = 1024
    out = pl.pallas_call(
        _mod_kernel,
        grid=(depth, n // tn),
        in_specs=[pl.BlockSpec((rows, d), lambda l, j: (0, 0)),
                  pl.BlockSpec((1, d, tn), lambda l, j: (l, 0, j)),
                  pl.BlockSpec((1, 1, tn), lambda l, j: (l, 0, j))],
        out_specs=pl.BlockSpec((1, rows, tn), lambda l, j: (l, 0, j)),
        out_shape=jax.ShapeDtypeStruct((depth, rows, n), F32),
        compiler_params=_cparams(("arbitrary", "arbitrary")),
        name="adaln_mod",
    )(cp, ada_w, ada_b.reshape(depth, 1, n))
    return out[:, :b]


def _norm_kernel(x_ref, g_ref, sc_ref, sh_ref, o_ref):
    x = x_ref[0]
    y = x * lax.rsqrt(jnp.mean(x * x, axis=-1, keepdims=True) + RMS_EPS) * g_ref[...]
    o_ref[0] = (y * (1.0 + sc_ref[0]) + sh_ref[0]).astype(o_ref.dtype)


def _norm_mod(x, g, sc, sh):
    b, s, d = x.shape
    ts = min(512, s)
    return pl.pallas_call(
        _norm_kernel,
        grid=(b, s // ts),
        in_specs=[pl.BlockSpec((1, ts, d), lambda i, j: (i, j, 0)),
                  pl.BlockSpec((1, d), lambda i, j: (0, 0)),
                  pl.BlockSpec((1, 1, d), lambda i, j: (i, 0, 0)),
                  pl.BlockSpec((1, 1, d), lambda i, j: (i, 0, 0))],
        out_specs=pl.BlockSpec((1, ts, d), lambda i, j: (i, j, 0)),
        out_shape=jax.ShapeDtypeStruct((b, s, d), BF16),
        compiler_params=_cparams(("parallel", "parallel")),
        name="norm_mod",
    )(x, g.reshape(1, d), sc.reshape(b, 1, d), sh.reshape(b, 1, d))


def _group_rms_store(acc, gain_ref, o_ref, group):
    tn = acc.shape[1]
    for s in range(tn // LANES):
        sl = slice(s * LANES, (s + 1) * LANES)
        a = acc[:, sl]
        sq = a * a
        if group == LANES:
            ms = jnp.sum(sq, axis=-1, keepdims=True) * (1.0 / LANES)
        else:
            lo = lax.broadcasted_iota(jnp.int32, a.shape, 1) < group
            s_lo = jnp.sum(jnp.where(lo, sq, 0.0), axis=-1, keepdims=True)
            s_hi = jnp.sum(jnp.where(lo, 0.0, sq), axis=-1, keepdims=True)
            ms = jnp.where(lo, s_lo, s_hi) * (1.0 / group)
        y = a * lax.rsqrt(ms + RMS_EPS) * gain_ref[:, sl]
        o_ref[:, sl] = y.astype(o_ref.dtype)


def _proj_kernel(h_ref, w_ref, o_ref):
    o_ref[...] = jnp.dot(h_ref[...], w_ref[...], preferred_element_type=F32).astype(o_ref.dtype)


def _proj(h2d, w, out_dtype):
    t, d = h2d.shape
    n = w.shape[1]
    tm = min(1024, t)
    tn = min(512, n)
    return pl.pallas_call(
        _proj_kernel,
        grid=(t // tm, n // tn),
        in_specs=[pl.BlockSpec((tm, d), lambda i, j: (i, 0)),
                  pl.BlockSpec((d, tn), lambda i, j: (0, j))],
        out_specs=pl.BlockSpec((tm, tn), lambda i, j: (i, j)),
        out_shape=jax.ShapeDtypeStruct((t, n), out_dtype),
        compiler_params=_cparams(("parallel", "arbitrary")),
        name="proj",
    )(h2d, w.astype(BF16))


PROJ_TN = 512


def _proj_mixers_kernel(h_ref, w_ref, gain_ref, attn_ref, gdn_ref, *, roles):
    j = pl.program_id(1)
    acc = jnp.dot(h_ref[...], w_ref[...], preferred_element_type=F32)

    def tiles_with(role):
        cond = None
        for n, r in enumerate(roles):
            if r == role:
                cond = (j == n) if cond is None else (cond | (j == n))
        return cond

    @pl.when(tiles_with("rms128"))
    def _():
        _group_rms_store(acc, gain_ref, attn_ref, LANES)

    @pl.when(tiles_with("rms64"))
    def _():
        _group_rms_store(acc, gain_ref, attn_ref, LANES // 2)

    @pl.when(tiles_with("bf16"))
    def _():
        attn_ref[...] = acc.astype(attn_ref.dtype)

    @pl.when(tiles_with("f32"))
    def _():
        gdn_ref[...] = acc


def _proj_mixers(h2d, w, gain, roles):
    t, d = h2d.shape
    n = w.shape[1]
    tm = min(1024, t)
    tn = PROJ_TN
    n_attn = sum(r != "f32" for r in roles)
    assert len(roles) * tn == n and all(r == "f32" for r in roles[n_attn:])
    return pl.pallas_call(
        functools.partial(_proj_mixers_kernel, roles=tuple(roles)),
        grid=(t // tm, len(roles)),
        in_specs=[pl.BlockSpec((tm, d), lambda i, j: (i, 0)),
                  pl.BlockSpec((d, tn), lambda i, j: (0, j)),
                  pl.BlockSpec((1, tn), lambda i, j: (0, j))],
        out_specs=[pl.BlockSpec((tm, tn), lambda i, j: (i, jnp.minimum(j, n_attn - 1))),
                   pl.BlockSpec((tm, tn), lambda i, j: (i, jnp.maximum(j - n_attn, 0)))],
        out_shape=[jax.ShapeDtypeStruct((t, n_attn * tn), BF16),
                   jax.ShapeDtypeStruct((t, n - n_attn * tn), F32)],
        compiler_params=_cparams(("parallel", "arbitrary")),
        name="proj_mixers",
    )(h2d, w.astype(BF16), gain.reshape(1, n).astype(F32))


NA_ROWS_PER_STEP = 8
NA_BAND = NA_WIN_R * GRID_W


def _na_bias_table(rpb):
    h = rpb.shape[0]
    w = jnp.arange(GRID_W)
    col_start = jnp.clip(w - NA_WIN_C // 2, 0, GRID_W - NA_WIN_C)
    ck = jnp.arange(GRID_W)
    inwin = (ck[None, :] >= col_start[:, None]) & (ck[None, :] < col_start[:, None] + NA_WIN_C)
    cbi = jnp.clip(ck[None, :] - w[:, None] + (NA_WIN_C - 1), 0, 2 * NA_WIN_C - 2)
    e = jnp.arange(NA_WIN_R)
    j = jnp.arange(NA_WIN_R)
    rbi = j[None, :] + (NA_WIN_R - 1) - e[:, None]
    tab = rpb[:, rbi]
    tab = tab[:, :, :, cbi]
    tab = jnp.where(inwin[None, None, None], tab, NEG_BIG)
    return tab.transpose(1, 0, 3, 2, 4).reshape(NA_WIN_R, h, GRID_W, NA_BAND).astype(F32)


def _na_kernel(q_ref, kp_ref, kc_ref, kn_ref, vp_ref, vc_ref, vn_ref, tab_ref, o_ref,
               kbuf, vbuf, *, rows):
    rb = pl.program_id(1)
    blk = NA_ROWS_PER_STEP * GRID_W
    for n, (kr, vr) in enumerate(((kp_ref, vp_ref), (kc_ref, vc_ref), (kn_ref, vn_ref))):
        kbuf[n * blk:(n + 1) * blk, :] = kr[0]
        vbuf[n * blk:(n + 1) * blk, :] = vr[0]
    heads = [slice(h * HEAD_DIM, (h + 1) * HEAD_DIM) for h in range(NA_HEADS)]

    def band(i):
        r = rb * NA_ROWS_PER_STEP + i
        r_start = jnp.clip(r - NA_WIN_R // 2, 0, rows - NA_WIN_R)
        loc = pl.multiple_of((r_start - (rb - 1) * NA_ROWS_PER_STEP) * GRID_W, GRID_W)
        return r - r_start, loc, slice(i * GRID_W, (i + 1) * GRID_W)

    def scores(i):
        e, loc, rs = band(i)
        return [lax.dot_general(q_ref[0, rs, hs], kbuf[pl.ds(loc, NA_BAND), hs], _NT,
                                preferred_element_type=F32) + tab_ref[e, h]
                for h, hs in enumerate(heads)]

    def finish(i, s_heads):
        _, loc, rs = band(i)
        for hs, s in zip(heads, s_heads):
            p = jnp.exp(s - jnp.max(s, axis=-1, keepdims=True))
            l = jnp.sum(p, axis=-1, keepdims=True)
            o = jnp.dot(p.astype(BF16), vbuf[pl.ds(loc, NA_BAND), hs], preferred_element_type=F32) / l
            o_ref[0, rs, hs] = o.astype(o_ref.dtype)

    pending = scores(0)
    for i in range(1, NA_ROWS_PER_STEP):
        nxt = scores(i)
        finish(i - 1, pending)
        pending = nxt
    finish(NA_ROWS_PER_STEP - 1, pending)


def _natten(qkv, rpb):
    b, s, _ = qkv.shape
    rows = s // GRID_W
    blk = NA_ROWS_PER_STEP * GRID_W
    nrb = s // blk
    tab = _na_bias_table(rpb)

    def at(off, col):
        return lambda i, j: (i, jnp.clip(j + off, 0, nrb - 1), col)

    return pl.pallas_call(
        functools.partial(_na_kernel, rows=rows),
        grid=(b, nrb),
        in_specs=[pl.BlockSpec((1, blk, NA_W), at(0, 0)),
                  pl.BlockSpec((1, blk, NA_W), at(-1, 1)),
                  pl.BlockSpec((1, blk, NA_W), at(0, 1)),
                  pl.BlockSpec((1, blk, NA_W), at(1, 1)),
                  pl.BlockSpec((1, blk, NA_W), at(-1, 2)),
                  pl.BlockSpec((1, blk, NA_W), at(0, 2)),
                  pl.BlockSpec((1, blk, NA_W), at(1, 2)),
                  pl.BlockSpec(tab.shape, lambda i, j: (0, 0, 0, 0))],
        out_specs=pl.BlockSpec((1, blk, NA_W), at(0, 0)),
        out_shape=jax.ShapeDtypeStruct((b, s, NA_W), BF16),
        scratch_shapes=[pltpu.VMEM((3 * blk, NA_W), BF16), pltpu.VMEM((3 * blk, NA_W), BF16)],
        compiler_params=_cparams(("parallel", "parallel")),
        name="natten",
    )(*([qkv] * 7), tab)


DA_TILE = 512
DA_BLOCK = 256
DA_HEADS_PER_STEP = 2
LOG2E = math.log2(math.e)


def _da_slopes():
    return [LOG2E * 2.0 ** (-8.0 * (i + 1) / DA_HEADS) for i in range(DA_HEADS)]


def _bf16_parts(x):
    p1 = x.astype(BF16)
    r1 = x - p1.astype(F32)
    p2 = r1.astype(BF16)
    p3 = (r1 - p2.astype(F32)).astype(BF16)
    return [p1, p2, p3]


def _da_pos_tables(tile):
    slopes = jnp.asarray(_da_slopes(), F32)[:, None]
    pos = jnp.arange(tile)
    parts = _bf16_parts(slopes * pos.astype(F32)[None, :])
    ones = [jnp.ones((DA_HEADS, tile), BF16)] * 3
    zpad = jnp.zeros((DA_HEADS, DA_QK_DIM - 6, tile), BF16)
    kfeat = jnp.concatenate([jnp.stack(parts + ones, axis=1), zpad], axis=1)
    qfeat = jnp.concatenate([jnp.stack(ones + [-p for p in parts], axis=1), zpad], axis=1)
    dist = jnp.abs(pos[:, None] - pos[None, :]).astype(F32)
    bdiag = -slopes[:, :, None] * dist[None]
    return qfeat, kfeat, bdiag


def _da_kernel(qt_ref, ka_ref, vt_ref, qpos_ref, bdiag_ref, slope_ref, lam_ref, g_ref,
               o_ref, qa_sc, st0_sc, st1_sc, mt0_sc, mt1_sc, m_sc, l_sc, acc_sc, *, tile, lam_init):
    hg = pl.program_id(1)
    qi = pl.program_id(2)
    nk = vt_ref.shape[2]
    half = DA_QK_DIM
    nprob = 2 * DA_HEADS_PER_STEP
    slopes =[slope_ref[hg * DA_HEADS_PER_STEP + hh] for hh in range(DA_HEADS_PER_STEP)]

    for hh in range(DA_HEADS_PER_STEP):
        qt = qt_ref[0, hh * LANES:(hh + 1) * LANES]
        qpos = qpos_ref[hh]
        for sel, pp in enumerate((qpos, -qpos, jnp.zeros_like(qpos))):
            qa_sc[sel, 2 * hh] = jnp.concatenate([qt[0:half], pp], axis=0)
            qa_sc[sel, 2 * hh + 1] = jnp.concatenate([pp, qt[half:2 * half]], axis=0)

    def keys(p, start, size):
        return ka_ref[0, p // 2, p % 2, pl.ds(start, size), :]

    def values(p, t):
        return vt_ref[0, p // 2, t]

    m_sc[...] = jnp.full(m_sc.shape, NEG_BIG, F32)
    acc_sc[...] = jnp.zeros(acc_sc.shape, F32)
    l_sc[...] = jnp.zeros(l_sc.shape, F32)
    blocks = [slice(i, i + DA_BLOCK) for i in range(0, tile, DA_BLOCK)]

    def stage_tile(s):
        after = (s - 1 >= qi).astype(jnp.int32)
        return s - 1 + after, after

    def score_units(t, sel, st_sc, mt_sc, diag=False):
        for p in range(nprob):
            for qs in blocks:
                mcol = None
                for ks in blocks:
                    start = pl.multiple_of(t * tile + ks.start, DA_BLOCK)
                    st = jnp.dot(keys(p, start, DA_BLOCK), qa_sc[sel, p, :, qs],
                                 preferred_element_type=F32)
                    if diag:
                        st = st + bdiag_ref[p // 2, ks, qs]
                    st_sc[p, ks, qs] = st
                    mk = jnp.max(st, axis=0, keepdims=True)
                    mcol = mk if mcol is None else jnp.maximum(mcol, mk)
                    if ks is blocks[-1]:
                        mt_sc[p, :, qs] = mcol
                    yield

    def softmax_units(t, st_sc, mt_sc):
        dist = (tile * jnp.abs(t - qi)).astype(F32)
        stats = []
        for p in range(nprob):
            shift = -slopes[p // 2] * dist
            m_old = m_sc[p]
            m_new = jnp.maximum(m_old, mt_sc[p] + shift)
            stats.append((jnp.exp2(m_old - m_new), m_new - shift))
            m_sc[p] = m_new
        for p, (alpha, msub) in enumerate(stats):
            vat = values(p, t)
            for qs in blocks:
                acc = alpha[:, qs] * acc_sc[p, :, qs]
                den = alpha[:, qs] * l_sc[p, :, qs]
                for ks in blocks:
                    pt = jnp.exp2(st_sc[p, ks, qs] - msub[:, qs])
                    den = den + jnp.sum(pt, axis=0, keepdims=True)
                    acc = acc + jnp.dot(vat[:, ks], pt.astype(BF16), preferred_element_type=F32)
                    if ks is blocks[-1]:
                        acc_sc[p, :, qs] = acc
                        l_sc[p, :, qs] = den
                    yield

    def overlapped(scores, softmax):
        if scores is not None:
            next(scores)
        for _ in softmax:
            if scores is not None:
                next(scores, None)

    bufs = ((st0_sc, mt0_sc), (st1_sc, mt1_sc))
    overlapped(None, score_units(qi, 2, *bufs[0], diag=True))
    if nk == 1:
        overlapped(None, softmax_units(qi, *bufs[0]))
    else:
        overlapped(score_units(*stage_tile(1), *bufs[1]), softmax_units(qi, *bufs[0]))

        def body(s, carry):
            for parity in range(2):
                @pl.when((s & 1) == parity)
                def _():
                    overlapped(score_units(*stage_tile(s + 1), *bufs[1 - parity]),
                               softmax_units(stage_tile(s)[0], *bufs[parity]))
            return carry

        lax.fori_loop(1, nk - 1, body, 0)
        overlapped(None, softmax_units(stage_tile(nk - 1)[0], *bufs[(nk - 1) % 2]))

    lp = lam_ref[...]
    lam = (jnp.exp(jnp.sum(lp[0:1] * lp[1:2], axis=-1, keepdims=True))
           - jnp.exp(jnp.sum(lp[2:3] * lp[3:4], axis=-1, keepdims=True)) + lam_init)
    for hh in range(DA_HEADS_PER_STEP):
        o0 = acc_sc[2 * hh] / l_sc[2 * hh]
        o1 = acc_sc[2 * hh + 1] / l_sc[2 * hh + 1]
        a = o0 - lam * o1
        y = a * lax.rsqrt(jnp.mean(a * a, axis=0, keepdims=True) + RMS_EPS) * g_ref[...]
        o_ref[0, :, hh * DA_V_DIM:(hh + 1) * DA_V_DIM] = (y * (1.0 - lam_init)).T.astype(o_ref.dtype)


def _diff_attn(qk, v, lam_p, subln_g, lam_init):
    b, s, _ = v.shape
    tile = min(DA_TILE, s)
    nt = s // tile
    nh = DA_HEADS
    qfeat, kfeat, bdiag = _da_pos_tables(tile)
    slopes = jnp.asarray(_da_slopes(), F32)

    qt = qk[:, :, :DA_QK_W].transpose(0, 2, 1)
    k4 = qk[:, :, DA_QK_W:].reshape(b, s, nh, 2, DA_QK_DIM)
    kp = jnp.broadcast_to(jnp.tile(kfeat.transpose(0, 2, 1), (1, nt, 1))[None], (b, nh, s, DA_QK_DIM))
    k0 = jnp.concatenate([k4[:, :, :, 0].transpose(0, 2, 1, 3), kp], axis=-1)
    k1 = jnp.concatenate([kp, k4[:, :, :, 1].transpose(0, 2, 1, 3)], axis=-1)
    ka = jnp.stack([k0, k1], axis=2)
    vt = v.reshape(b, nt, tile, nh, DA_V_DIM).transpose(0, 3, 1, 4, 2)

    hps = DA_HEADS_PER_STEP
    nprob = 2 * hps
    resident = pl.Buffered(1)
    return pl.pallas_call(
        functools.partial(_da_kernel, tile=tile, lam_init=lam_init),
        grid=(b, nh // hps, nt),
        in_specs=[pl.BlockSpec((1, hps * LANES, tile), lambda i, h, q: (i, h, q)),
                  pl.BlockSpec((1, hps, 2, s, LANES), lambda i, h, q: (i, h, 0, 0, 0),
                               pipeline_mode=resident),
                  pl.BlockSpec((1, hps, nt, DA_V_DIM, tile), lambda i, h, q: (i, h, 0, 0, 0),
                               pipeline_mode=resident),
                  pl.BlockSpec((hps, DA_QK_DIM, tile), lambda i, h, q: (h, 0, 0)),
                  pl.BlockSpec((hps, tile, tile), lambda i, h, q: (h, 0, 0)),
                  pl.BlockSpec(memory_space=pltpu.SMEM),
                  pl.BlockSpec((4, DA_QK_DIM), lambda i, h, q: (0, 0)),
                  pl.BlockSpec((DA_V_DIM, 1), lambda i, h, q: (0, 0))],
        out_specs=pl.BlockSpec((1, tile, hps * DA_V_DIM), lambda i, h, q: (i, q, h)),
        out_shape=jax.ShapeDtypeStruct((b, s, DA_V_W), BF16),
        scratch_shapes=[pltpu.VMEM((3, nprob, LANES, tile), BF16),
                        pltpu.VMEM((nprob, tile, tile), F32),
                        pltpu.VMEM((nprob, tile, tile), F32),
                        pltpu.VMEM((nprob, 1, tile), F32),
                        pltpu.VMEM((nprob, 1, tile), F32),
                        pltpu.VMEM((nprob, 1, tile), F32),
                        pltpu.VMEM((nprob, 1, tile), F32),
                        pltpu.VMEM((nprob, DA_V_DIM, tile), F32)],
        compiler_params=_cparams(("parallel", "parallel", "arbitrary")),
        name="diff_attn",
    )(qt, ka, vt, qfeat, bdiag, slopes, lam_p.astype(F32),
      subln_g.reshape(DA_V_DIM, 1).astype(F32))


def _gdn_gate_kernel(lg_ref, alog_ref, dt_ref, beta_ref, gc_ref, gt_ref):
    lg = lg_ref[0]
    nh = 2 * GDN_HEADS
    beta_ref[0] = _sigmoid(lg[0:nh])
    z = lg[nh:2 * nh] + dt_ref[...]
    sp = jnp.maximum(z, 0.0) + jnp.log(1.0 + jnp.exp(-jnp.abs(z)))
    g = -jnp.exp(alog_ref[...]) * sp
    ts = g.shape[1]
    pos = lax.broadcasted_iota(jnp.int32, g.shape, 1) & (GDN_CHUNK - 1)
    pre = g
    suf = g
    sh = 1
    while sh < GDN_CHUNK:
        pre = pre + jnp.where(pos >= sh, pltpu.roll(pre, sh, axis=1), 0.0)
        suf = suf + jnp.where(pos < GDN_CHUNK - sh, pltpu.roll(suf, ts - sh, axis=1), 0.0)
        sh *= 2
    bwd = lax.broadcasted_iota(jnp.int32, g.shape, 0) >= GDN_HEADS
    gc_ref[0] = jnp.where(bwd, suf, pre)
    gt_ref[0] = jnp.where(bwd, pre, suf) - g


def _gdn_gates(lgt, a_log, dt_bias):
    b, _, s = lgt.shape
    nh = 2 * GDN_HEADS
    ts = min(2048, s)
    shp = jax.ShapeDtypeStruct((b, nh, s), F32)
    spec = pl.BlockSpec((1, nh, ts), lambda i, j: (i, 0, j))
    return pl.pallas_call(
        _gdn_gate_kernel,
        grid=(b, s // ts),
        in_specs=[pl.BlockSpec((1, 2 * nh, ts), lambda i, j: (i, 0, j)),
                  pl.BlockSpec((nh, 1), lambda i, j: (0, 0)),
                  pl.BlockSpec((nh, 1), lambda i, j: (0, 0))],
        out_specs=[spec, spec, spec],
        out_shape=[shp, shp, shp],
        compiler_params=_cparams(("parallel", "parallel")),
        name="gdn_gates",
    )(lgt, a_log.reshape(nh, 1).astype(F32), dt_bias.reshape(nh, 1).astype(F32))


GDN_HALO = 8


def _gdn_conv_kernel(x_ref, p_ref, n_ref, w_ref, o_ref, ext, *, ts):
    cg = pl.program_id(0)
    i = pl.program_id(2)
    last = pl.num_programs(2) - 1
    ext[0:GDN_HALO, :] = jnp.where(i > 0, p_ref[0], 0.0)
    ext[GDN_HALO:GDN_HALO + ts, :] = x_ref[0]
    ext[GDN_HALO + ts:2 * GDN_HALO + ts, :] = jnp.where(i < last, n_ref[0], 0.0)
    y = None
    for tap in range(GDN_CONV):
        off = GDN_HALO + tap - GDN_CONV // 2
        term = ext[off:off + ts, :] * w_ref[0, tap:tap + 1, :]
        y = term if y is None else y + term
    y = y * _sigmoid(y)
    qscale = jnp.where(cg == 0, GDN_DK ** -0.5, 1.0).astype(F32)
    is_v = cg == 2
    for h in range(GDN_HEADS):
        hs = slice(h * GDN_DK, (h + 1) * GDN_DK)
        a = y[:, hs]
        nrm = lax.rsqrt(jnp.sum(a * a, axis=-1, keepdims=True) + RMS_EPS) * qscale
        o_ref[0, 0, :, hs] = a * jnp.where(is_v, 1.0, nrm)


def _gdn_conv(u, conv_w):
    b, s, _ = u.shape
    cw = GDN_K_W
    ts = min(512, s)
    nb = s // GDN_HALO
    per = ts // GDN_HALO
    w3 = conv_w.reshape(GDN_CONV, 3, cw).transpose(1, 0, 2)
    w3 = jnp.pad(w3, ((0, 0), (0, 8 - GDN_CONV), (0, 0))).astype(F32)
    return pl.pallas_call(
        functools.partial(_gdn_conv_kernel, ts=ts),
        grid=(3, b, s // ts),
        in_specs=[pl.BlockSpec((1, ts, cw), lambda c, i, j: (i, j, c)),
                  pl.BlockSpec((1, GDN_HALO, cw),
                               lambda c, i, j: (i, jnp.maximum(j * per - 1, 0), c)),
                  pl.BlockSpec((1, GDN_HALO, cw),
                               lambda c, i, j: (i, jnp.minimum((j + 1) * per, nb - 1), c)),
                  pl.BlockSpec((1, 8, cw), lambda c, i, j: (c, 0, 0))],
        out_specs=pl.BlockSpec((1, 1, ts, cw), lambda c, i, j: (c, i, j, 0)),
        out_shape=jax.ShapeDtypeStruct((3, b, s, cw), F32),
        scratch_shapes=[pltpu.VMEM((ts + 2 * GDN_HALO, cw), F32)],
        compiler_params=_cparams(("parallel", "parallel", "parallel")),
        name="gdn_conv",
    )(u, u, u, w3)


class _GdnChain(NamedTuple):
    q: jax.Array
    k: jax.Array
    v: jax.Array
    cols: jax.Array
    gr: jax.Array
    rev: bool
    state: object
    out: object


def _gdn_chains(chains):
    sc = GDN_SUPER
    ck = GDN_CHUNK
    nchunk = sc // ck
    row = lax.broadcasted_iota(jnp.int32, (ck, sc), 0)
    lane = lax.broadcasted_iota(jnp.int32, (ck, sc), 1)
    col = lane & (ck - 1)
    lane_chunk = lane >> _LOG2_CHUNK
    eye = jnp.where(row == col, 1.0, 0.0).astype(F32)
    masks = {False: (row >= col, row > col), True: (row <= col, row < col)}
    ii = lax.broadcasted_iota(jnp.int32, (sc, sc), 0)
    jj = lax.broadcasted_iota(jnp.int32, (sc, sc), 1)
    bd_mask = jnp.where((ii >> _LOG2_CHUNK) == (jj >> _LOG2_CHUNK), 1.0, 0.0).astype(BF16)

    def pack(full):
        out = full[0:ck]
        for b in range(1, nchunk):
            out = jnp.where(lane_chunk == b, full[b * ck:(b + 1) * ck], out)
        return out

    def blockdiag(packed_bf16):
        return jnp.concatenate([packed_bf16] * nchunk, axis=0) * bd_mask

    def lanes(c):
        return jnp.broadcast_to(c, (sc, LANES))

    bcast = [(lanes(ch.cols[:, 0:1]), lanes(ch.cols[:, 1:2]), lanes(ch.cols[:, 2:3])) for ch in chains]

    decays, kbs, a_mats = [], [], []
    for ch, (beta, gc, _) in zip(chains, bcast):
        incl, strict = masks[ch.rev]
        gc2 = jnp.concatenate([gc, gc], axis=1)
        decay = jnp.exp(jnp.where(incl, pack(gc2) - ch.gr, NEG_BIG))
        kb = ch.k * beta
        decays.append(decay)
        kbs.append(kb)
        a_mats.append(jnp.where(strict, pack(_dot_nt(kb, ch.k)) * decay, 0.0))

    xs = [-a for a in a_mats]
    ts = [eye + x for x in xs]
    xs = [_dot(x, blockdiag(x.astype(BF16))) for x in xs]
    span = 2
    while span < ck:
        last = 2 * span >= ck
        new_ts, new_xs = [], []
        for t, x in zip(ts, xs):
            w_bd = blockdiag(x.astype(BF16))
            if last:
                new_ts.append(t + _dot(t, w_bd))
                new_xs.append(None)
            else:
                both = _dot(jnp.concatenate([t, x], axis=0), w_bd)
                new_ts.append(t + both[0:ck])
                new_xs.append(both[ck:2 * ck])
        ts, xs = new_ts, new_xs
        span *= 2
    resids = []
    for a, t in zip(a_mats, ts):
        a_hi = a.astype(BF16)
        a_lo = (a - a_hi.astype(F32)).astype(BF16)
        t_hi = t.astype(BF16)
        t_lo = (t - t_hi.astype(F32)).astype(BF16)
        hi = jnp.dot(jnp.concatenate([a_hi, a_lo], axis=0), blockdiag(t_hi), preferred_element_type=F32)
        a_t = hi[0:ck] + hi[ck:2 * ck] + jnp.dot(a_hi, blockdiag(t_lo), preferred_element_type=F32)
        resids.append(eye - t - a_t)
    ts = [t + _dot(t, blockdiag(r.astype(BF16))) for t, r in zip(ts, resids)]

    pre = []
    for ch, (beta, gc, gt), decay, kb, t in zip(chains, bcast, decays, kbs, ts):
        egc = jnp.exp(gc)
        uw = _dot(blockdiag(t.astype(BF16)), jnp.concatenate([ch.v * beta, kb * egc], axis=1))
        pre.append(dict(u=uw[:, 0:GDN_DV], w=uw[:, GDN_DV:], qg=ch.q * egc,
                        ai=pack(_dot_nt(ch.q, ch.k)) * decay, kt_t=(ch.k * jnp.exp(gt)).T,
                        gl=jnp.exp(gc + gt)))

    states = [ch.state[...] for ch in chains]
    outs = [[None] * nchunk for _ in chains]
    for step in range(nchunk):
        order = [nchunk - 1 - step if ch.rev else step for ch in chains]
        spans = [slice(j * ck, (j + 1) * ck) for j in order]
        from_state = [_dot(jnp.concatenate([p["w"][sl], p["qg"][sl]], axis=0), s_mat)
                      for p, sl, s_mat in zip(pre, spans, states)]
        v_news = [p["u"][sl] - r[0:ck] for p, sl, r in zip(pre, spans, from_state)]
        from_v = [_dot(jnp.concatenate([p["ai"][:, sl], p["kt_t"][:, sl]], axis=0), v_new)
                  for p, sl, v_new in zip(pre, spans, v_news)]
        for n, (p, j, r1, r2) in enumerate(zip(pre, order, from_state, from_v)):
            outs[n][j] = r1[ck:2 * ck] + r2[0:ck]
            states[n] = states[n] * p["gl"][j * ck:j * ck + 1, :] + r2[ck:]
    for ch, s_mat, o in zip(chains, states, outs):
        ch.state[...] = s_mat
        ch.out[...] = jnp.concatenate(o, axis=0)


GDN_HEADS_PER_STEP = 8


def _gdn_scan_kernel(qf_ref, kf_ref, vf_ref, cf_ref, rf_ref, qb_ref, kb_ref, vb_ref, cb_ref, rb_ref,
                     of_ref, ob_ref, state):
    hg = pl.program_id(1)

    @pl.when(pl.program_id(2) == 0)
    def _():
        state[...] = jnp.zeros(state.shape, F32)

    per_group = 3 * GDN_HEADS_PER_STEP
    shift = (LANES - per_group * hg) & (LANES - 1)
    cols_fb = (pltpu.roll(cf_ref[0, 0], shift, axis=1), pltpu.roll(cb_ref[0, 0], shift, axis=1))

    chains = []
    for i in range(GDN_HEADS_PER_STEP):
        hd = hg * GDN_HEADS_PER_STEP + i
        ks = slice(i * GDN_DK, (i + 1) * GDN_DK)
        vs = slice(i * GDN_DV, (i + 1) * GDN_DV)
        for rev, (q_ref, k_ref, v_ref, r_ref, o_ref) in enumerate(
                ((qf_ref, kf_ref, vf_ref, rf_ref, of_ref), (qb_ref, kb_ref, vb_ref, rb_ref, ob_ref))):
            chains.append(_GdnChain(
                q=q_ref[0, 0, :, ks], k=k_ref[0, 0, :, ks], v=v_ref[0, 0, :, vs],
                cols=cols_fb[rev][:, 3 * i:3 * i + 3],
                gr=r_ref[0, 0, pl.ds(hd, 1), :], rev=bool(rev),
                state=state.at[2 * i + rev], out=o_ref.at[0, :, vs]))
    _gdn_chains(chains)


def _gdn_scan(qkv, cols, rows):
    _, b, s, _ = qkv.shape
    sc = GDN_SUPER
    nsc = s // sc
    hps = GDN_HEADS_PER_STEP

    def specs(d):
        seq = (lambda n: nsc - 1 - n) if d else (lambda n: n)
        return [pl.BlockSpec((1, 1, sc, hps * GDN_DK), lambda i, h, n: (0, i, seq(n), h)),
                pl.BlockSpec((1, 1, sc, hps * GDN_DK), lambda i, h, n: (1, i, seq(n), h)),
                pl.BlockSpec((1, 1, sc, hps * GDN_DV), lambda i, h, n: (2, i, seq(n), h)),
                pl.BlockSpec((1, 1, sc, LANES), lambda i, h, n: (i, d, seq(n), 0)),
                pl.BlockSpec((1, 1, GDN_HEADS, sc), lambda i, h, n: (i, d, 0, seq(n)))]

    out_shape = jax.ShapeDtypeStruct((b, s, GDN_V_W), F32)
    return pl.pallas_call(
        _gdn_scan_kernel,
        grid=(b, GDN_HEADS // hps, nsc),
        in_specs=specs(0) + specs(1),
        out_specs=[pl.BlockSpec((1, sc, hps * GDN_DV), lambda i, h, n: (i, n, h)),
                   pl.BlockSpec((1, sc, hps * GDN_DV), lambda i, h, n: (i, nsc - 1 - n, h))],
        out_shape=[out_shape, out_shape],
        scratch_shapes=[pltpu.VMEM((2 * hps, GDN_DK, GDN_DV), F32)],
        compiler_params=_cparams(("parallel", "parallel", "arbitrary")),
        name="gdn_scan",
    )(qkv, qkv, qkv, cols, rows, qkv, qkv, qkv, cols, rows)


def _gdn_out_kernel(of_ref, ob_ref, z_ref, g_ref, o_ref):
    o = of_ref[0] + ob_ref[0]
    z = z_ref[0].astype(F32)
    gate = z * _sigmoid(z)
    for h in range(GDN_HEADS):
        hs = slice(h * GDN_DV, (h + 1) * GDN_DV)
        a = o[:, hs]
        y = a * lax.rsqrt(jnp.mean(a * a, axis=-1, keepdims=True) + RMS_EPS) * g_ref[...]
        o_ref[0, :, hs] = (y * gate[:, hs]).astype(o_ref.dtype)


def _gdn_out(o_f, o_b, u, norm_g):
    b, s, w = o_f.shape
    ts = min(512, s)
    spec = pl.BlockSpec((1, ts, w), lambda i, j: (i, j, 0))
    z_spec = pl.BlockSpec((1, ts, w), lambda i, j: (i, j, 3))
    return pl.pallas_call(
        _gdn_out_kernel,
        grid=(b, s // ts),
        in_specs=[spec, spec, z_spec, pl.BlockSpec((1, GDN_DV), lambda i, j: (0, 0))],
        out_specs=spec,
        out_shape=jax.ShapeDtypeStruct((b, s, w), BF16),
        compiler_params=_cparams(("parallel", "parallel")),
        name="gdn_out",
    )(o_f, o_b, u, norm_g.reshape(1, GDN_DV).astype(F32))


def _gated_deltanet(u, lg, conv_w, a_log, dt_bias, norm_g):
    b, s, _ = u.shape
    nh = GDN_HEADS
    lgt = lg[:, :, :4 * nh].transpose(0, 2, 1)
    beta, gc, gt = _gdn_gates(lgt, a_log, dt_bias)
    rows = gc.reshape(b, 2, nh, s)
    cols = jnp.stack([beta, gc, gt], axis=-1).reshape(b, 2, nh, s, 3)
    cols = cols.transpose(0, 1, 3, 2, 4).reshape(b, 2, s, 3 * nh)
    cols = jnp.pad(cols, ((0, 0), (0, 0), (0, 0), (0, LANES - 3 * nh)))
    qkv = _gdn_conv(u, conv_w)
    o_f, o_b = _gdn_scan(qkv, cols, rows)
    return _gdn_out(o_f, o_b, u, norm_g)


def _merge_kernel(h_ref, ya_ref, yb_ref, yc_ref, gw0_ref, gw1_ref, gw2_ref,
                  gb0_ref, gb1_ref, gb2_ref, wa_ref, wb_ref, wc_ref, wo_ref,
                  x_ref, gt_ref, o_ref, acc):
    j = pl.program_id(1)

    @pl.when(j == 0)
    def _():
        acc[...] = jnp.zeros(acc.shape, F32)

    h = h_ref[...]

    def gate(gw_ref, gb_ref):
        return _sigmoid(jnp.dot(h, gw_ref[...], preferred_element_type=F32) + gb_ref[...])

    m = (gate(gw0_ref, gb0_ref) * jnp.dot(ya_ref[...], wa_ref[...], preferred_element_type=F32)
         + gate(gw1_ref, gb1_ref) * jnp.dot(yb_ref[...], wb_ref[...], preferred_element_type=F32)
         + gate(gw2_ref, gb2_ref) * jnp.dot(yc_ref[...], wc_ref[...], preferred_element_type=F32))
    acc[...] += jnp.dot(m.astype(BF16), wo_ref[...], preferred_element_type=F32)

    @pl.when(j == pl.num_programs(1) - 1)
    def _():
        o_ref[...] = x_ref[...] + gt_ref[0] * acc[...]


def _merge(h2d, ya, yb, yc, gate_w, gate_b, wa, wb, wc, w_out, x2d, gt, seq):
    t, d = x2d.shape
    tm = min(512, seq)
    tn = 512
    nj = d // tn
    per_batch = seq // tm
    gw = gate_w.astype(BF16)
    gb = gate_b.reshape(1, N_BRANCHES * d).astype(F32)
    row = lambda w: pl.BlockSpec((tm, w), lambda i, j: (i, 0))
    gws = [pl.BlockSpec((d, tn), (lambda i, j, r=r: (0, r * nj + j))) for r in range(N_BRANCHES)]
    gbs = [pl.BlockSpec((1, tn), (lambda i, j, r=r: (0, r * nj + j))) for r in range(N_BRANCHES)]
    return pl.pallas_call(
        _merge_kernel,
        grid=(t // tm, nj),
        in_specs=[row(d), row(ya.shape[1]), row(yb.shape[1]), row(yc.shape[1]),
                  *gws, *gbs,
                  pl.BlockSpec((wa.shape[0], tn), lambda i, j: (0, j)),
                  pl.BlockSpec((wb.shape[0], tn), lambda i, j: (0, j)),
                  pl.BlockSpec((wc.shape[0], tn), lambda i, j: (0, j)),
                  pl.BlockSpec((tn, d), lambda i, j: (j, 0)),
                  row(d),
                  pl.BlockSpec((1, 1, d), lambda i, j: (i // per_batch, 0, 0))],
        out_specs=row(d),
        out_shape=jax.ShapeDtypeStruct((t, d), F32),
        scratch_shapes=[pltpu.VMEM((tm, d), F32)],
        compiler_params=_cparams(("parallel", "arbitrary")),
        name="merge",
    )(h2d, ya, yb, yc, gw, gw, gw, gb, gb, gb,
      wa.astype(BF16), wb.astype(BF16), wc.astype(BF16), w_out.astype(BF16), x2d, gt)


def _ffn_kernel(h_ref, w1_ref, w3_ref, w2_ref, x_ref, gt_ref, o_ref, acc):
    j = pl.program_id(1)

    @pl.when(j == 0)
    def _():
        acc[...] = jnp.zeros(acc.shape, F32)

    h = h_ref[...]
    a = jnp.dot(h, w1_ref[...], preferred_element_type=F32)
    b = jnp.dot(h, w3_ref[...], preferred_element_type=F32)
    acc[...] += jnp.dot((a * _sigmoid(a) * b).astype(BF16), w2_ref[...], preferred_element_type=F32)

    @pl.when(j == pl.num_programs(1) - 1)
    def _():
        o_ref[...] = x_ref[...] + gt_ref[0] * acc[...]


def _ffn(h2d, w1, w3, w2, x2d, gt, seq):
    t, d = x2d.shape
    f = w1.shape[1]
    tm = min(512, seq)
    tf = 512
    per_batch = seq // tm
    row = pl.BlockSpec((tm, d), lambda i, j: (i, 0))
    return pl.pallas_call(
        _ffn_kernel,
        grid=(t // tm, f // tf),
        in_specs=[row,
                  pl.BlockSpec((d, tf), lambda i, j: (0, j)),
                  pl.BlockSpec((d, tf), lambda i, j: (0, j)),
                  pl.BlockSpec((tf, d), lambda i, j: (j, 0)),
                  row,
                  pl.BlockSpec((1, 1, d), lambda i, j: (i // per_batch, 0, 0))],
        out_specs=row,
        out_shape=jax.ShapeDtypeStruct((t, d), F32),
        scratch_shapes=[pltpu.VMEM((tm, d), F32)],
        compiler_params=_cparams(("parallel", "arbitrary")),
        name="ffn",
    )(h2d, w1.astype(BF16), w3.astype(BF16), w2.astype(BF16), x2d, gt)


def _project_inputs(h2d, w_in, na_qnorm_g, na_knorm_g, da_qnorm_g, da_knorm_g):
    segments = ((2 * NA_W, "rms128"), (NA_W, "bf16"), (2 * DA_QK_W, "rms64"), (DA_V_W, "bf16"),
                (2 * GDN_K_W + 2 * GDN_V_W, "f32"))
    roles = [role for width, role in segments for _ in range(width // PROJ_TN)]
    n_main = sum(width for width, _ in segments)
    gain = jnp.concatenate([jnp.tile(na_qnorm_g * HEAD_DIM ** -0.5, NA_HEADS),
                            jnp.tile(na_knorm_g, NA_HEADS),
                            jnp.ones((NA_W,), F32),
                            jnp.tile(da_qnorm_g * (LOG2E * DA_QK_DIM ** -0.5), 2 * DA_HEADS),
                            jnp.tile(da_knorm_g, 2 * DA_HEADS),
                            jnp.ones((DA_V_W + 2 * GDN_K_W + 2 * GDN_V_W,), F32)])
    attn, gdn_u = _proj_mixers(h2d, w_in[:, :n_main], gain, roles)
    w_lg = jnp.pad(w_in[:, n_main:], ((0, 0), (0, LANES - 4 * GDN_HEADS)))
    return attn, gdn_u, _proj(h2d, w_lg, F32)


def _layer(l, x, mod, norm1_g, norm2_g, w_in, gate_w, gate_b, na_qnorm_g, na_knorm_g, na_rpb,
           da_qnorm_g, da_knorm_g, da_lambda, da_subln_g, gdn_conv_w, gdn_a_log, gdn_dt_bias,
           gdn_norm_g, w_branch_a, w_branch_b, w_branch_c, w_out, ffn_w1, ffn_w3, ffn_w2):
    b, s, d = x.shape
    t = b * s
    sh1, sc1, gt1, sh2, sc2, gt2 = [mod[:, i * d:(i + 1) * d] for i in range(6)]

    h = _norm_mod(x, norm1_g, sc1, sh1)
    h2d = h.reshape(t, d)

    attn, gdn_u, gd_lg = _project_inputs(h2d, w_in, na_qnorm_g, na_knorm_g, da_qnorm_g, da_knorm_g)
    attn = attn.reshape(b, s, -1)

    ya = _natten(attn, na_rpb)
    lam_init = 0.8 - 0.6 * math.exp(-0.3 * l)
    da0 = 3 * NA_W
    yb = _diff_attn(attn[:, :, da0:da0 + 2 * DA_QK_W], attn[:, :, da0 + 2 * DA_QK_W:],
                    da_lambda, da_subln_g, lam_init)
    yc = _gated_deltanet(gdn_u.reshape(b, s, -1), gd_lg.reshape(b, s, -1), gdn_conv_w, gdn_a_log,
                         gdn_dt_bias, gdn_norm_g)

    x2d = _merge(h2d, ya.reshape(t, -1), yb.reshape(t, -1), yc.reshape(t, -1),
                 gate_w, gate_b, w_branch_a, w_branch_b, w_branch_c, w_out,
                 x.reshape(t, d), gt1.reshape(b, 1, d), s)

    h2 = _norm_mod(x2d.reshape(b, s, d), norm2_g, sc2, sh2)
    x2d = _ffn(h2.reshape(t, d), ffn_w1, ffn_w3, ffn_w2, x2d, gt2.reshape(b, 1, d), s)
    return x2d.reshape(b, s, d)


def kernel(x, c, ada_w, ada_b, norm1_g, norm2_g, w_in, gate_w, gate_b, na_qnorm_g, na_knorm_g, na_rpb, da_qnorm_g, da_knorm_g, da_lambda, da_subln_g, gdn_conv_w, gdn_a_log, gdn_dt_bias, gdn_norm_g, w_branch_a, w_branch_b, w_branch_c, w_out, ffn_w1, ffn_w3, ffn_w2):
    depth = ada_w.shape[0]
    mod = _adaln(c, ada_w, ada_b)
    per_layer = (norm1_g, norm2_g, w_in, gate_w, gate_b, na_qnorm_g, na_knorm_g, na_rpb,
                 da_qnorm_g, da_knorm_g, da_lambda, da_subln_g, gdn_conv_w, gdn_a_log,
                 gdn_dt_bias, gdn_norm_g, w_branch_a, w_branch_b, w_branch_c, w_out,
                 ffn_w1, ffn_w3, ffn_w2)
    for l in range(depth):
        x = _layer(l, x, mod[l], *[p[l] for p in per_layer])
    return x
```
